```python
import math
import jax, jax.numpy as jnp
from jax import lax
import numpy as np

D_MODEL = 1024
BATCH = 8
SEQ = 4096
DEPTH = 1

MIX_WIDTH = D_MODEL
SGU_WIDTH = MIX_WIDTH // 2
SGU_HEADS = 4
SGU_HEAD_DIM = SGU_WIDTH // SGU_HEADS
SGU_CHUNK = 128
RET_WIDTH = MIX_WIDTH - SGU_WIDTH
RET_HEADS = 4
RET_HEAD_DIM = RET_WIDTH // RET_HEADS
RET_CHUNK = 128
ROPE_BASE = 10000.0
IN_WIDTH = 2 * SGU_WIDTH + 4 * RET_WIDTH
IN_SPLITS = (SGU_WIDTH, 2 * SGU_WIDTH, 2 * SGU_WIDTH + RET_WIDTH,
             2 * SGU_WIDTH + 2 * RET_WIDTH, 2 * SGU_WIDTH + 3 * RET_WIDTH)
PEER_HEADS = 8
PEER_N_KEYS = 128
PEER_N_EXPERTS = PEER_N_KEYS * PEER_N_KEYS
PEER_TOPK = 16
PEER_QUERY_DIM = 256
PEER_KEY_DIM = PEER_QUERY_DIM // 2
PEER_TOKEN_BLOCK = 128
LN_EPS = 1e-5
GN_EPS = 1e-5
DEEPNORM_ALPHA = (2.0 * DEPTH) ** 0.25
DEEPNORM_BETA = (8.0 * DEPTH) ** -0.25

kernel_name = "hybrid_sgu_retention_peer_deepnorm"


def _layer_norm(x, w, b):
    xf = x.astype(jnp.float32)
    mu = xf.mean(-1, keepdims=True)
    var = jnp.square(xf - mu).mean(-1, keepdims=True)
    return ((xf - mu) * lax.rsqrt(var + LN_EPS) * w + b).astype(x.dtype)


def _rotary(x):
    s, dh = x.shape[1], x.shape[-1]
    half = dh // 2
    inv = ROPE_BASE ** (-jnp.arange(half, dtype=jnp.float32) / half)
    ang = jnp.arange(s, dtype=jnp.float32)[:, None] * inv[None, :]
    cos = jnp.cos(ang)[:, None, :]
    sin = jnp.sin(ang)[:, None, :]
    x1, x2 = x[..., :half], x[..., half:]
    return jnp.concatenate([x1 * cos - x2 * sin, x1 * sin + x2 * cos], -1).astype(x.dtype)


def _spatial_gating_group(u, v, ln_w, ln_b, w_s, b_s):
    u = jax.nn.gelu(u, approximate=False)
    v = _layer_norm(jax.nn.gelu(v, approximate=False), ln_w, ln_b)
    b, s, w = v.shape
    c = s // SGU_CHUNK
    vr = v.reshape(b, c, SGU_CHUNK, SGU_HEADS, SGU_HEAD_DIM)
    mixed = jnp.einsum('hlm,bcmhd->bclhd', w_s, vr) + b_s.T[None, None, :, :, None]
    return u * mixed.reshape(b, s, w).astype(u.dtype)


def _retention_one_direction(q, k, v, log_g, include_diag):
    b, h, s, dk = q.shape
    dv = v.shape[-1]
    L = RET_CHUNK
    c = s // L
    q = q.reshape(b, h, c, L, dk)
    k = k.reshape(b, h, c, L, dk)
    v = v.reshape(b, h, c, L, dv)
    pos = jnp.arange(L, dtype=jnp.float32)
    diff = pos[:, None] - pos[None, :]
    mask = (diff >= 0) if include_diag else (diff > 0)
    decay = jnp.where(mask[None], jnp.exp(jnp.maximum(diff, 0.0)[None] * log_g[:, None, None]), 0.0)
    scores = jnp.einsum('bhcld,bhcmd->bhclm', q, k) * decay[None, :, None]
    o_inner = jnp.einsum('bhclm,bhcmv->bhclv', scores, v)
    k_w = jnp.exp((L - 1 - pos)[None, :] * log_g[:, None])
    chunk_state = jnp.einsum('bhcld,bhclv->bhcdv', k * k_w[None, :, None, :, None], v)
    g_chunk = jnp.exp(L * log_g)[:, None, None]

    def step(state, s_c):
        return g_chunk * state + s_c, state

    _, prev = lax.scan(step, jnp.zeros_like(chunk_state[:, :, 0]), jnp.moveaxis(chunk_state, 2, 0))
    prev = jnp.moveaxis(prev, 0, 2)
    q_w = jnp.exp((pos + 1.0)[None, :] * log_g[:, None])
    o_cross = jnp.einsum('bhcld,bhcdv->bhclv', q * q_w[None, :, None, :, None], prev)
    return (o_inner + o_cross).reshape(b, h, s, dv)


def _retention_group(q, k, v, g, decay_fwd, decay_bwd, gn_w):
    b, s, _ = q.shape
    out_dtype = q.dtype
    q = _rotary(q.reshape(b, s, RET_HEADS, RET_HEAD_DIM))
    k = _rotary(k.reshape(b, s, RET_HEADS, RET_HEAD_DIM)) * (RET_HEAD_DIM ** -0.5)
    v = v.reshape(b, s, RET_HEADS, RET_HEAD_DIM)
    q, k, v = (t.transpose(0, 2, 1, 3) for t in (q, k, v))
    lg_f = jax.nn.log_sigmoid(decay_fwd.astype(jnp.float32))
    lg_b = jax.nn.log_sigmoid(decay_bwd.astype(jnp.float32))
    o_f = _retention_one_direction(q, k, v, lg_f, True)
    flip = lambda t: jnp.flip(t, axis=2)
    o_b = flip(_retention_one_direction(flip(q), flip(k), flip(v), lg_b, False))
    o = (o_f + o_b).transpose(0, 2, 1, 3).astype(jnp.float32)
    mu = o.mean(-1, keepdims=True)
    var = jnp.square(o - mu).mean(-1, keepdims=True)
    on = ((o - mu) * lax.rsqrt(var + GN_EPS)).reshape(b, s, RET_WIDTH) * gn_w
    return (jax.nn.silu(g.astype(jnp.float32)) * on).astype(out_dtype)


def _peer(h, w_q, sub_keys, u_tab, v_tab):
    b, s, d = h.shape
    q = (h @ w_q).reshape(b, s, PEER_HEADS, 2, PEER_KEY_DIM)
    scores = jnp.einsum('bshpd,hpnd->bshpn', q, sub_keys)
    top_s, top_i = lax.top_k(scores, PEER_TOPK)
    cand_s = top_s[..., 0, :, None] + top_s[..., 1, None, :]
    cand_i = top_i[..., 0, :, None] * PEER_N_KEYS + top_i[..., 1, None, :]
    cand_s = cand_s.reshape(b, s, PEER_HEADS, PEER_TOPK * PEER_TOPK)
    cand_i = cand_i.reshape(b, s, PEER_HEADS, PEER_TOPK * PEER_TOPK)
    best_s, best_pos = lax.top_k(cand_s, PEER_TOPK)
    expert_idx = jnp.take_along_axis(cand_i, best_pos, axis=-1)
    gates = jax.nn.softmax(best_s.astype(jnp.float32), axis=-1)
    n_blocks = (b * s) // PEER_TOKEN_BLOCK
    n_sel = PEER_HEADS * PEER_TOPK
    xb = h.reshape(n_blocks, PEER_TOKEN_BLOCK, d)
    ib = expert_idx.reshape(n_blocks, PEER_TOKEN_BLOCK, n_sel)
    gb = gates.reshape(n_blocks, PEER_TOKEN_BLOCK, n_sel).astype(h.dtype)

    def block(args):
        xt, it, gt = args
        act = jax.nn.gelu(jnp.einsum('td,ted->te', xt, u_tab[it]), approximate=False)
        return jnp.einsum('te,ted->td', gt * act, v_tab[it])

    out = lax.map(block, (xb, ib, gb))
    return out.reshape(b, s, d)


def setup_inputs(seed: int = 0) -> dict:
    key = jax.random.key(seed)
    ks = jax.random.split(key, 20)
    f32 = jnp.float32
    nrm = lambda k, shape, scale: jax.random.normal(k, shape, f32) * scale
    base_decay = jnp.asarray(np.log(2.0 ** (5.0 + np.arange(RET_HEADS)) - 1.0), f32)
    return {
        "x": jax.random.normal(ks[0], (BATCH, SEQ, D_MODEL), f32),
        "w_in": nrm(ks[1], (DEPTH, D_MODEL, IN_WIDTH), D_MODEL ** -0.5),
        "sgu_ln_w": 1.0 + nrm(ks[2], (DEPTH, SGU_WIDTH), 0.02),
        "sgu_ln_b": nrm(ks[3], (DEPTH, SGU_WIDTH), 0.02),
        "sgu_w_s": nrm(ks[4], (DEPTH, SGU_HEADS, SGU_CHUNK, SGU_CHUNK), SGU_CHUNK ** -0.5),
        "sgu_b": 1.0 + nrm(ks[5], (DEPTH, SGU_HEADS, SGU_CHUNK), 0.02),
        "ret_decay_fwd": base_decay[None] + nrm(ks[6], (DEPTH, RET_HEADS), 0.05),
        "ret_decay_bwd": base_decay[None] + nrm(ks[7], (DEPTH, RET_HEADS), 0.05),
        "ret_gn_w": 1.0 + nrm(ks[8], (DEPTH, RET_WIDTH), 0.02),
        "w_out": nrm(ks[9], (DEPTH, MIX_WIDTH, D_MODEL), DEEPNORM_BETA * MIX_WIDTH ** -0.5),
        "ln1_w": 1.0 + nrm(ks[10], (DEPTH, D_MODEL), 0.02),
        "ln1_b": nrm(ks[11], (DEPTH, D_MODEL), 0.02),
        "peer_w_q": nrm(ks[12], (DEPTH, D_MODEL, PEER_HEADS * PEER_QUERY_DIM), D_MODEL ** -0.5),
        "peer_sub_keys": nrm(ks[13], (DEPTH, PEER_HEADS, 2, PEER_N_KEYS, PEER_KEY_DIM), PEER_KEY_DIM ** -0.5),
        "peer_u": nrm(ks[14], (DEPTH, PEER_N_EXPERTS, D_MODEL), D_MODEL ** -0.5),
        "peer_v": nrm(ks[15], (DEPTH, PEER_N_EXPERTS, D_MODEL), DEEPNORM_BETA),
        "ln2_w": 1.0 + nrm(ks[16], (DEPTH, D_MODEL), 0.02),
        "ln2_b": nrm(ks[17], (DEPTH, D_MODEL), 0.02),
    }


def reference(x, w_in, sgu_ln_w, sgu_ln_b, sgu_w_s, sgu_b, ret_decay_fwd, ret_decay_bwd, ret_gn_w,
              w_out, ln1_w, ln1_b, peer_w_q, peer_sub_keys, peer_u, peer_v, ln2_w, ln2_b):
    for l in range(DEPTH):
        proj = x @ w_in[l]
        a_u, a_v, r_q, r_k, r_v, r_g = jnp.split(proj, IN_SPLITS, axis=-1)
        a_out = _spatial_gating_group(a_u, a_v, sgu_ln_w[l], sgu_ln_b[l], sgu_w_s[l], sgu_b[l])
        r_out = _retention_group(r_q, r_k, r_v, r_g, ret_decay_fwd[l], ret_decay_bwd[l], ret_gn_w[l])
        mixed = jnp.concatenate([a_out, r_out], axis=-1) @ w_out[l]
        h = _layer_norm(DEEPNORM_ALPHA * x + mixed, ln1_w[l], ln1_b[l])
        ffn = _peer(h, peer_w_q[l], peer_sub_keys[l], peer_u[l], peer_v[l])
        x = _layer_norm(DEEPNORM_ALPHA * h + ffn, ln2_w[l], ln2_b[l])
    return x
```

```python
import functools

import jax
import jax.numpy as jnp
from jax import lax
from jax.experimental import pallas as pl
from jax.experimental.pallas import tpu as pltpu

F32 = jnp.float32
BF16 = jnp.bfloat16

LANES = 128
SUBLANES = 8
CHUNK = 128
HEAD_DIM = 128
N_HEADS = 4
GROUP_W = N_HEADS * HEAD_DIM
ROPE_BASE = 10000.0
LN_EPS = 1e-5
GN_EPS = 1e-5
ALPHA = 2.0 ** 0.25

PEER_HEADS = 8
PEER_KEYS = 128
PEER_TOPK = 16
N_SEL = PEER_HEADS * PEER_TOPK
ROWS_PER_EXPERT = 4
PEER_T = 128
MIX_T = 256
ROUTE_T = 256
VMEM_LIMIT = 56 * 1024 * 1024


def _gelu(x):
    return 0.5 * x * (1.0 + lax.erf(x * (2.0 ** -0.5)))


def _layer_norm(x, w, b, eps):
    mu = jnp.mean(x, axis=-1, keepdims=True)
    xc = x - mu
    var = jnp.mean(xc * xc, axis=-1, keepdims=True)
    return xc * lax.rsqrt(var + eps) * w + b


def _dot(a, b):
    return jnp.dot(a, b, preferred_element_type=F32)


def _dot_nt(a, b):
    return lax.dot_general(a, b, (((1,), (1,)), ((), ())), preferred_element_type=F32)


def _dot_tn(a, b):
    return lax.dot_general(a, b, (((0,), (0,)), ((), ())), preferred_element_type=F32)


def _mix_front_kernel(x_ref, w_in_ref, sln_w_ref, sln_b_ref, ws_ref, bs_ref, cos_ref, sin_ref,
                      kwf_ref, kwb_ref, a_ref, r_ref, sf_ref, sb_ref):
    xb = x_ref[...].astype(BF16)
    chunks = MIX_T // CHUNK

    u = _gelu(_dot(xb, w_in_ref[:, 0:GROUP_W]))
    v = _gelu(_dot(xb, w_in_ref[:, GROUP_W:2 * GROUP_W]))
    vn = _layer_norm(v, sln_w_ref[...], sln_b_ref[...], LN_EPS).astype(BF16)
    for c in range(chunks):
        rows = slice(c * CHUNK, (c + 1) * CHUNK)
        for h in range(N_HEADS):
            cols = slice(h * HEAD_DIM, (h + 1) * HEAD_DIM)
            mixed = _dot(ws_ref[h], vn[rows, cols]) + bs_ref[h]
            a_ref[rows, cols] = (u[rows, cols] * mixed).astype(a_ref.dtype)

    cos = cos_ref[...]
    sin = sin_ref[...]
    q = _dot(xb, w_in_ref[:, 2 * GROUP_W:3 * GROUP_W])
    k = _dot(xb, w_in_ref[:, 3 * GROUP_W:4 * GROUP_W])
    vr = _dot(xb, w_in_ref[:, 4 * GROUP_W:5 * GROUP_W])
    g = _dot(xb, w_in_ref[:, 5 * GROUP_W:6 * GROUP_W])
    r_ref[:, 2 * GROUP_W:3 * GROUP_W] = vr.astype(r_ref.dtype)
    r_ref[:, 3 * GROUP_W:4 * GROUP_W] = (g * jax.nn.sigmoid(g)).astype(r_ref.dtype)
    for h in range(N_HEADS):
        cols = slice(h * HEAD_DIM, (h + 1) * HEAD_DIM)
        qh = q[:, cols]
        kh = k[:, cols]
        qh = qh * cos + pltpu.roll(qh, HEAD_DIM // 2, 1) * sin
        kh = (kh * cos + pltpu.roll(kh, HEAD_DIM // 2, 1) * sin) * (HEAD_DIM ** -0.5)
        r_ref[:, cols] = qh.astype(r_ref.dtype)
        r_ref[:, GROUP_W + h * HEAD_DIM:GROUP_W + (h + 1) * HEAD_DIM] = kh.astype(r_ref.dtype)
        for c in range(chunks):
            rows = slice(c * CHUNK, (c + 1) * CHUNK)
            vc = vr[rows, cols].astype(BF16)
            sf_ref[c, h] = _dot_tn((kh[rows] * kwf_ref[h]).astype(BF16), vc)
            sb_ref[c, h] = _dot_tn((kh[rows] * kwb_ref[h]).astype(BF16), vc)


def _ret_scan_kernel(sf_ref, sb_ref, gf_ref, gb_ref, pf_ref, pb_ref):
    n_chunks = sf_ref.shape[0]
    gf = gf_ref[...]
    gb = gb_ref[...]

    def fwd(c, st):
        pf_ref[c] = st
        return gf * st + sf_ref[c]

    lax.fori_loop(0, n_chunks, fwd, jnp.zeros((HEAD_DIM, HEAD_DIM), F32))

    def bwd(i, st):
        c = n_chunks - 1 - i
        pb_ref[c] = st
        return gb * st + sb_ref[c]

    lax.fori_loop(0, n_chunks, bwd, jnp.zeros((HEAD_DIM, HEAD_DIM), F32))


def _mix_back_kernel(x_ref, a_ref, r_ref, pf_ref, pb_ref, dec_ref, qwf_ref, qwb_ref, gn_ref,
                     w_out_ref, ln_w_ref, ln_b_ref, h_ref, cat_ref):
    chunks = MIX_T // CHUNK
    cat_ref[:, 0:GROUP_W] = a_ref[...]
    for c in range(chunks):
        rows = slice(c * CHUNK, (c + 1) * CHUNK)
        for h in range(N_HEADS):
            cols = slice(h * HEAD_DIM, (h + 1) * HEAD_DIM)
            qh = r_ref[rows, cols]
            kh = r_ref[rows, GROUP_W + h * HEAD_DIM:GROUP_W + (h + 1) * HEAD_DIM]
            vh = r_ref[rows, 2 * GROUP_W + h * HEAD_DIM:2 * GROUP_W + (h + 1) * HEAD_DIM]
            gate = r_ref[rows, 3 * GROUP_W + h * HEAD_DIM:3 * GROUP_W + (h + 1) * HEAD_DIM].astype(F32)
            scores = _dot_nt(qh, kh) * dec_ref[h]
            o = _dot(scores.astype(BF16), vh)
            qf = qh.astype(F32)
            o = o + _dot((qf * qwf_ref[h]).astype(BF16), pf_ref[c, h].astype(BF16))
            o = o + _dot((qf * qwb_ref[h]).astype(BF16), pb_ref[c, h].astype(BF16))
            mu = jnp.mean(o, axis=-1, keepdims=True)
            oc = o - mu
            var = jnp.mean(oc * oc, axis=-1, keepdims=True)
            on = oc * lax.rsqrt(var + GN_EPS) * gn_ref[:, cols]
            cat_ref[rows, GROUP_W + h * HEAD_DIM:GROUP_W + (h + 1) * HEAD_DIM] = (gate * on).astype(cat_ref.dtype)
    mixed = _dot(cat_ref[...], w_out_ref[...])
    h_ref[...] = _layer_norm(ALPHA * x_ref[...] + mixed, ln_w_ref[...], ln_b_ref[...], LN_EPS)


def _top_rows(s, k):
    n_rows = s.shape[0]
    row = lax.broadcasted_iota(jnp.int32, s.shape, 0)
    vals, idxs = [], []
    for _ in range(k):
        m = jnp.max(s, axis=0, keepdims=True)
        i = jnp.min(jnp.where(s == m, row, n_rows), axis=0, keepdims=True)
        vals.append(m)
        idxs.append(i)
        s = jnp.where(row == i, -jnp.inf, s)
    return jnp.concatenate(vals, axis=0), jnp.concatenate(idxs, axis=0)


def _select_rows(table, pos):
    out = jnp.zeros(pos.shape, table.dtype)
    for a in range(table.shape[0]):
        out = jnp.where(pos == a, table[a:a + 1, :], out)
    return out


def _peer_route_kernel(h_ref, wq_ref, keys_ref, idx_ref, gate_ref, q_ref):
    head = pl.program_id(1)

    @pl.when(head == 0)
    def _():
        q_ref[...] = _dot(h_ref[...].astype(BF16), wq_ref[...]).astype(q_ref.dtype)

    tops, topi = [], []
    for p in range(2):
        col = pl.multiple_of((head * 2 + p) * PEER_KEYS, PEER_KEYS)
        qs = q_ref[:, pl.ds(col, PEER_KEYS)]
        scores = _dot_nt(keys_ref[p], qs)
        ts, ti = _top_rows(scores, PEER_TOPK)
        tops.append(ts)
        topi.append(ti)
    cand = jnp.concatenate([tops[0][a:a + 1, :] + tops[1] for a in range(PEER_TOPK)], axis=0)
    best, pos = _top_rows(cand, PEER_TOPK)
    pos_a = lax.shift_right_logical(pos, 4)
    pos_b = jnp.bitwise_and(pos, PEER_TOPK - 1)
    expert = _select_rows(topi[0], pos_a) * PEER_KEYS + _select_rows(topi[1], pos_b)
    e = jnp.exp(best - best[0:1, :])
    idx_ref[...] = expert * ROWS_PER_EXPERT
    gate_ref[...] = e / jnp.sum(e, axis=0, keepdims=True)


def _pack_table(tab):
    e, d = tab.shape
    tb = tab.astype(BF16)
    lo = lax.bitcast_convert_type(tb[:, : d // 2], jnp.uint16).astype(jnp.uint32)
    hi = lax.bitcast_convert_type(tb[:, d // 2:], jnp.uint16).astype(jnp.uint32)
    w = lo | (hi << 16)
    return lax.bitcast_convert_type(w, jnp.int32).reshape(e * ROWS_PER_EXPERT, LANES)


def _slot_mask():
    c = lax.broadcasted_iota(jnp.int32, (SUBLANES, 8 * N_SEL), 1)
    i = lax.broadcasted_iota(jnp.int32, (SUBLANES, 8 * N_SEL), 0)
    return i == 4 * (c % 2) + (c % 8) // 2


def _gather_rows(idx_ref, t, tab_ref, g_ref):
    for e in range(N_SEL):
        off = pl.multiple_of(idx_ref[t, e], ROWS_PER_EXPERT)
        g_ref[ROWS_PER_EXPERT * e: ROWS_PER_EXPERT * (e + 1), :] = tab_ref[pl.ds(off, ROWS_PER_EXPERT), :]


def _peer_act_kernel(idx_ref, h_ref, gate_ref, tab_ref, w_ref, g_ref, r_ref):
    mask = _slot_mask()

    def body(t, carry):
        _gather_rows(idx_ref, t, tab_ref, g_ref)
        gb = pltpu.bitcast(g_ref[...], BF16)
        r = _dot_nt(h_ref[t].astype(BF16), gb)
        r_ref[pl.ds(t, 1), :] = jnp.sum(jnp.where(mask, r, 0.0), axis=0, keepdims=True)
        return carry

    lax.fori_loop(0, PEER_T, body, 0)
    c = lax.broadcasted_iota(jnp.int32, (8 * N_SEL, N_SEL), 0)
    e = lax.broadcasted_iota(jnp.int32, (8 * N_SEL, N_SEL), 1)
    gsel = (c // 8 == e).astype(F32)
    act = jnp.dot(r_ref[...], gsel, precision=lax.Precision.HIGHEST, preferred_element_type=F32)
    w_ref[...] = gate_ref[...] * _gelu(act)


def _peer_out_kernel(idx_ref, w_ref, tab_ref, o_ref, g_ref, wx_ref):
    mask = _slot_mask()
    e = lax.broadcasted_iota(jnp.int32, (N_SEL, 8 * N_SEL), 0)
    c = lax.broadcasted_iota(jnp.int32, (N_SEL, 8 * N_SEL), 1)
    expand = (c // 8 == e).astype(F32)
    wx_ref[...] = jnp.dot(w_ref[...], expand, precision=lax.Precision.HIGHEST, preferred_element_type=F32)

    def body(t, carry):
        _gather_rows(idx_ref, t, tab_ref, g_ref)
        gb = pltpu.bitcast(g_ref[...], BF16)
        wrow = jnp.broadcast_to(wx_ref[pl.ds(t, 1), :], (SUBLANES, 8 * N_SEL))
        w8 = jnp.where(mask, wrow, 0.0).astype(BF16)
        o_ref[t] = _dot(w8, gb)
        return carry

    lax.fori_loop(0, PEER_T, body, 0)


def _final_ln_kernel(h_ref, f_ref, w_ref, b_ref, o_ref):
    o_ref[...] = _layer_norm(ALPHA * h_ref[...] + f_ref[...], w_ref[...], b_ref[...], LN_EPS)


def _const_spec(shape):
    zeros = (0,) * len(shape)
    return pl.BlockSpec(shape, lambda *_: zeros)


def _lane_bcast(v):
    return jnp.broadcast_to(v[:, :, None], v.shape + (LANES,)).astype(F32)


def kernel(x, w_in, sgu_ln_w, sgu_ln_b, sgu_w_s, sgu_b, ret_decay_fwd, ret_decay_bwd, ret_gn_w, w_out, ln1_w, ln1_b, peer_w_q, peer_sub_keys, peer_u, peer_v, ln2_w, ln2_b):
    b, s, d = x.shape
    n = b * s
    n_chunks = s // CHUNK
    assert d == 2 * GROUP_W and s % MIX_T == 0 and n % PEER_T == 0 and n % ROUTE_T == 0
    x2 = x.reshape(n, d)
    row = lambda v: v.reshape(1, -1).astype(F32)

    half = HEAD_DIM // 2
    inv = ROPE_BASE ** (-jnp.arange(half, dtype=F32) / half)
    ang = jnp.arange(s, dtype=F32)[:, None] * inv[None, :]
    cos2 = jnp.concatenate([jnp.cos(ang), jnp.cos(ang)], axis=-1)
    sin2 = jnp.concatenate([-jnp.sin(ang), jnp.sin(ang)], axis=-1)
    lgf = jax.nn.log_sigmoid(ret_decay_fwd[0].astype(F32))
    lgb = jax.nn.log_sigmoid(ret_decay_bwd[0].astype(F32))
    pos = jnp.arange(CHUNK, dtype=F32)
    diff = pos[:, None] - pos[None, :]
    decay = jnp.where(diff[None] >= 0, jnp.exp(jnp.maximum(diff, 0.0)[None] * lgf[:, None, None]),
                      jnp.exp(jnp.maximum(-diff, 0.0)[None] * lgb[:, None, None]))
    kwf = _lane_bcast(jnp.exp((CHUNK - 1 - pos)[None, :] * lgf[:, None]))
    kwb = _lane_bcast(jnp.exp(pos[None, :] * lgb[:, None]))
    qwf = _lane_bcast(jnp.exp((pos + 1.0)[None, :] * lgf[:, None]))
    qwb = _lane_bcast(jnp.exp((CHUNK - pos)[None, :] * lgb[:, None]))
    gcf = jnp.broadcast_to(jnp.exp(CHUNK * lgf)[:, None, None], (N_HEADS, HEAD_DIM, HEAD_DIM)).astype(F32)
    gcb = jnp.broadcast_to(jnp.exp(CHUNK * lgb)[:, None, None], (N_HEADS, HEAD_DIM, HEAD_DIM)).astype(F32)

    mix_grid = (b, s // MIX_T)
    tok_spec = lambda w: pl.BlockSpec((MIX_T, w), lambda i, j: (i * (s // MIX_T) + j, 0))
    pos_spec = pl.BlockSpec((MIX_T, HEAD_DIM), lambda i, j: (j, 0))
    state_spec = pl.BlockSpec((None, MIX_T // CHUNK, N_HEADS, HEAD_DIM, HEAD_DIM), lambda i, j: (i, j, 0, 0, 0))
    head_tab = _const_spec((N_HEADS, CHUNK, HEAD_DIM))
    mix_params = pltpu.CompilerParams(dimension_semantics=("arbitrary", "arbitrary"), vmem_limit_bytes=VMEM_LIMIT)
    state_shape = jax.ShapeDtypeStruct((b, n_chunks, N_HEADS, HEAD_DIM, HEAD_DIM), F32)

    a_out, r_pack, s_f, s_b = pl.pallas_call(
        _mix_front_kernel,
        grid=mix_grid,
        in_specs=[tok_spec(d), _const_spec(w_in.shape[1:]), _const_spec((1, GROUP_W)), _const_spec((1, GROUP_W)),
                  head_tab, head_tab, pos_spec, pos_spec, head_tab, head_tab],
        out_specs=[tok_spec(GROUP_W), tok_spec(4 * GROUP_W), state_spec, state_spec],
        out_shape=[jax.ShapeDtypeStruct((n, GROUP_W), BF16), jax.ShapeDtypeStruct((n, 4 * GROUP_W), BF16),
                   state_shape, state_shape],
        compiler_params=mix_params,
        name="mix_front",
    )(x2, w_in[0].astype(BF16), row(sgu_ln_w[0]), row(sgu_ln_b[0]), sgu_w_s[0].astype(BF16),
      _lane_bcast(sgu_b[0]), cos2, sin2, kwf, kwb)

    scan_state = pl.BlockSpec((None, n_chunks, None, HEAD_DIM, HEAD_DIM), lambda i, j: (i, 0, j, 0, 0))
    scan_tab = pl.BlockSpec((None, HEAD_DIM, HEAD_DIM), lambda i, j: (j, 0, 0))
    p_f, p_b = pl.pallas_call(
        _ret_scan_kernel,
        grid=(b, N_HEADS),
        in_specs=[scan_state, scan_state, scan_tab, scan_tab],
        out_specs=[scan_state, scan_state],
        out_shape=[state_shape, state_shape],
        compiler_params=mix_params,
        name="ret_scan",
    )(s_f, s_b, gcf, gcb)

    h = pl.pallas_call(
        _mix_back_kernel,
        grid=mix_grid,
        in_specs=[tok_spec(d), tok_spec(GROUP_W), tok_spec(4 * GROUP_W), state_spec, state_spec,
                  head_tab, head_tab, head_tab, _const_spec((1, GROUP_W)), _const_spec(w_out.shape[1:]),
                  _const_spec((1, d)), _const_spec((1, d))],
        out_specs=tok_spec(d),
        out_shape=jax.ShapeDtypeStruct((n, d), F32),
        scratch_shapes=[pltpu.VMEM((MIX_T, d), BF16)],
        compiler_params=mix_params,
        name="mix_back",
    )(x2, a_out, r_pack, p_f, p_b, decay.astype(F32), qwf, qwb, row(ret_gn_w[0]), w_out[0].astype(BF16),
      row(ln1_w[0]), row(ln1_b[0]))

    q_width = peer_w_q.shape[-1]
    idx_t, gate_t = pl.pallas_call(
        _peer_route_kernel,
        grid=(n // ROUTE_T, PEER_HEADS),
        in_specs=[pl.BlockSpec((ROUTE_T, d), lambda i, j: (i, 0)),
                  _const_spec((d, q_width)),
                  pl.BlockSpec((None, 2, PEER_KEYS, PEER_KEYS), lambda i, j: (j, 0, 0, 0))],
        out_specs=[pl.BlockSpec((PEER_TOPK, ROUTE_T), lambda i, j: (j, i)),
                   pl.BlockSpec((PEER_TOPK, ROUTE_T), lambda i, j: (j, i))],
        out_shape=[jax.ShapeDtypeStruct((N_SEL, n), jnp.int32), jax.ShapeDtypeStruct((N_SEL, n), F32)],
        scratch_shapes=[pltpu.VMEM((ROUTE_T, q_width), BF16)],
        compiler_params=mix_params,
        name="peer_route",
    )(h, peer_w_q[0].astype(BF16), peer_sub_keys[0].astype(BF16))
    idx4 = idx_t.T
    gates = gate_t.T

    steps = n // PEER_T
    u_pack = _pack_table(peer_u[0])
    v_pack = _pack_table(peer_v[0])
    tab_spec = pl.BlockSpec(u_pack.shape, lambda i: (0, 0), pipeline_mode=pl.Buffered(1))
    idx_spec = pl.BlockSpec((PEER_T, N_SEL), lambda i: (i, 0), memory_space=pltpu.SMEM)
    sel_spec = pl.BlockSpec((PEER_T, N_SEL), lambda i: (i, 0))
    tok3_spec = pl.BlockSpec((PEER_T, SUBLANES, LANES), lambda i: (i, 0, 0))
    peer_params = pltpu.CompilerParams(dimension_semantics=("arbitrary",), vmem_limit_bytes=VMEM_LIMIT)
    peer_scratch = [pltpu.VMEM((ROWS_PER_EXPERT * N_SEL, LANES), jnp.int32), pltpu.VMEM((PEER_T, 8 * N_SEL), F32)]
    weights = pl.pallas_call(
        _peer_act_kernel,
        grid=(steps,),
        in_specs=[idx_spec, tok3_spec, sel_spec, tab_spec],
        out_specs=sel_spec,
        out_shape=jax.ShapeDtypeStruct((n, N_SEL), F32),
        scratch_shapes=peer_scratch,
        compiler_params=peer_params,
        name="peer_act",
    )(idx4, h.reshape(n, SUBLANES, LANES), gates, u_pack)
    ffn = pl.pallas_call(
        _peer_out_kernel,
        grid=(steps,),
        in_specs=[idx_spec, sel_spec, tab_spec],
        out_specs=tok3_spec,
        out_shape=jax.ShapeDtypeStruct((n, SUBLANES, LANES), F32),
        scratch_shapes=peer_scratch,
        compiler_params=peer_params,
        name="peer_out",
    )(idx4, weights, v_pack)

    out = pl.pallas_call(
        _final_ln_kernel,
        grid=(n // MIX_T,),
        in_specs=[pl.BlockSpec((MIX_T, d), lambda i: (i, 0)), pl.BlockSpec((MIX_T, d), lambda i: (i, 0)),
                  _const_spec((1, d)), _const_spec((1, d))],
        out_specs=pl.BlockSpec((MIX_T, d), lambda i: (i, 0)),
        out_shape=jax.ShapeDtypeStruct((n, d), F32),
        compiler_params=peer_params,
        name="final_ln",
    )(h, ffn.reshape(n, d), row(ln2_w[0]), row(ln2_b[0]))
    return out.reshape(b, s, d)
```

```python
import functools

import jax
import jax.numpy as jnp
from jax import lax
from jax.experimental import pallas as pl
from jax.experimental.pallas import tpu as pltpu

F32 = jnp.float32
BF16 = jnp.bfloat16

LANES = 128
SUBLANES = 8
CHUNK = 128
HEAD_DIM = 128
N_HEADS = 4
GROUP_W = N_HEADS * HEAD_DIM
ROPE_BASE = 10000.0
LN_EPS = 1e-5
GN_EPS = 1e-5
ALPHA = 2.0 ** 0.25

PEER_HEADS = 8
PEER_KEYS = 128
PEER_TOPK = 16
N_SEL = PEER_HEADS * PEER_TOPK
ROWS_PER_EXPERT = 4
PEER_T = 128
PEER_GROUP = 8
MIX_T = 256
ROUTE_T = 256
VMEM_LIMIT = 56 * 1024 * 1024


def _gelu(x):
    return 0.5 * x * (1.0 + lax.erf(x * (2.0 ** -0.5)))


def _layer_norm(x, w, b, eps):
    mu = jnp.mean(x, axis=-1, keepdims=True)
    xc = x - mu
    var = jnp.mean(xc * xc, axis=-1, keepdims=True)
    return xc * lax.rsqrt(var + eps) * w + b


def _dot(a, b):
    return jnp.dot(a, b, preferred_element_type=F32)


def _dot_nt(a, b):
    return lax.dot_general(a, b, (((1,), (1,)), ((), ())), preferred_element_type=F32)


def _dot_tn(a, b):
    return lax.dot_general(a, b, (((0,), (0,)), ((), ())), preferred_element_type=F32)


def _mix_front_kernel(x_ref, w_in_ref, sln_w_ref, sln_b_ref, ws_ref, bs_ref, cos_ref, sin_ref,
                      kwf_ref, kwb_ref, a_ref, r_ref, sf_ref, sb_ref):
    xb = x_ref[...].astype(BF16)
    chunks = MIX_T // CHUNK

    u = _gelu(_dot(xb, w_in_ref[:, 0:GROUP_W]))
    v = _gelu(_dot(xb, w_in_ref[:, GROUP_W:2 * GROUP_W]))
    vn = _layer_norm(v, sln_w_ref[...], sln_b_ref[...], LN_EPS).astype(BF16)
    for c in range(chunks):
        rows = slice(c * CHUNK, (c + 1) * CHUNK)
        for h in range(N_HEADS):
            cols = slice(h * HEAD_DIM, (h + 1) * HEAD_DIM)
            mixed = _dot(ws_ref[h], vn[rows, cols]) + bs_ref[h]
            a_ref[rows, cols] = (u[rows, cols] * mixed).astype(a_ref.dtype)

    cos = cos_ref[...]
    sin = sin_ref[...]
    q = _dot(xb, w_in_ref[:, 2 * GROUP_W:3 * GROUP_W])
    k = _dot(xb, w_in_ref[:, 3 * GROUP_W:4 * GROUP_W])
    vr = _dot(xb, w_in_ref[:, 4 * GROUP_W:5 * GROUP_W])
    g = _dot(xb, w_in_ref[:, 5 * GROUP_W:6 * GROUP_W])
    r_ref[:, 2 * GROUP_W:3 * GROUP_W] = vr.astype(r_ref.dtype)
    r_ref[:, 3 * GROUP_W:4 * GROUP_W] = (g * jax.nn.sigmoid(g)).astype(r_ref.dtype)
    for h in range(N_HEADS):
        cols = slice(h * HEAD_DIM, (h + 1) * HEAD_DIM)
        qh = q[:, cols]
        kh = k[:, cols]
        qh = qh * cos + pltpu.roll(qh, HEAD_DIM // 2, 1) * sin
        kh = (kh * cos + pltpu.roll(kh, HEAD_DIM // 2, 1) * sin) * (HEAD_DIM ** -0.5)
        r_ref[:, cols] = qh.astype(r_ref.dtype)
        r_ref[:, GROUP_W + h * HEAD_DIM:GROUP_W + (h + 1) * HEAD_DIM] = kh.astype(r_ref.dtype)
        for c in range(chunks):
            rows = slice(c * CHUNK, (c + 1) * CHUNK)
            vc = vr[rows, cols].astype(BF16)
            sf_ref[c, h] = _dot_tn((kh[rows] * kwf_ref[h]).astype(BF16), vc)
            sb_ref[c, h] = _dot_tn((kh[rows] * kwb_ref[h]).astype(BF16), vc)


def _ret_scan_kernel(sf_ref, sb_ref, gf_ref, gb_ref, pf_ref, pb_ref):
    n_chunks = sf_ref.shape[0]
    gf = gf_ref[...]
    gb = gb_ref[...]

    def fwd(c, st):
        pf_ref[c] = st
        return gf * st + sf_ref[c]

    lax.fori_loop(0, n_chunks, fwd, jnp.zeros((HEAD_DIM, HEAD_DIM), F32))

    def bwd(i, st):
        c = n_chunks - 1 - i
        pb_ref[c] = st
        return gb * st + sb_ref[c]

    lax.fori_loop(0, n_chunks, bwd, jnp.zeros((HEAD_DIM, HEAD_DIM), F32))


def _mix_back_kernel(x_ref, a_ref, r_ref, pf_ref, pb_ref, dec_ref, qwf_ref, qwb_ref, gn_ref,
                     w_out_ref, ln_w_ref, ln_b_ref, h_ref, cat_ref):
    chunks = MIX_T // CHUNK
    cat_ref[:, 0:GROUP_W] = a_ref[...]
    for c in range(chunks):
        rows = slice(c * CHUNK, (c + 1) * CHUNK)
        for h in range(N_HEADS):
            cols = slice(h * HEAD_DIM, (h + 1) * HEAD_DIM)
            qh = r_ref[rows, cols]
            kh = r_ref[rows, GROUP_W + h * HEAD_DIM:GROUP_W + (h + 1) * HEAD_DIM]
            vh = r_ref[rows, 2 * GROUP_W + h * HEAD_DIM:2 * GROUP_W + (h + 1) * HEAD_DIM]
            gate = r_ref[rows, 3 * GROUP_W + h * HEAD_DIM:3 * GROUP_W + (h + 1) * HEAD_DIM].astype(F32)
            scores = _dot_nt(qh, kh) * dec_ref[h]
            o = _dot(scores.astype(BF16), vh)
            qf = qh.astype(F32)
            o = o + _dot((qf * qwf_ref[h]).astype(BF16), pf_ref[c, h].astype(BF16))
            o = o + _dot((qf * qwb_ref[h]).astype(BF16), pb_ref[c, h].astype(BF16))
            mu = jnp.mean(o, axis=-1, keepdims=True)
            oc = o - mu
            var = jnp.mean(oc * oc, axis=-1, keepdims=True)
            on = oc * lax.rsqrt(var + GN_EPS) * gn_ref[:, cols]
            cat_ref[rows, GROUP_W + h * HEAD_DIM:GROUP_W + (h + 1) * HEAD_DIM] = (gate * on).astype(cat_ref.dtype)
    mixed = _dot(cat_ref[...], w_out_ref[...])
    h_ref[...] = _layer_norm(ALPHA * x_ref[...] + mixed, ln_w_ref[...], ln_b_ref[...], LN_EPS)


def _top_rows(s, k):
    n_rows = s.shape[0]
    row = lax.broadcasted_iota(jnp.int32, s.shape, 0)
    vals, idxs = [], []
    for _ in range(k):
        m = jnp.max(s, axis=0, keepdims=True)
        i = jnp.min(jnp.where(s == m, row, n_rows), axis=0, keepdims=True)
        vals.append(m)
        idxs.append(i)
        s = jnp.where(row == i, -jnp.inf, s)
    return jnp.concatenate(vals, axis=0), jnp.concatenate(idxs, axis=0)


def _select_rows(table, pos):
    out = jnp.zeros(pos.shape, table.dtype)
    for a in range(table.shape[0]):
        out = jnp.where(pos == a, table[a:a + 1, :], out)
    return out


def _peer_route_kernel(h_ref, wq_ref, keys_ref, idx_ref, gate_ref, q_ref):
    head = pl.program_id(1)

    @pl.when(head == 0)
    def _():
        q_ref[...] = _dot(h_ref[...].astype(BF16), wq_ref[...]).astype(q_ref.dtype)

    tops, topi = [], []
    for p in range(2):
        col = pl.multiple_of((head * 2 + p) * PEER_KEYS, PEER_KEYS)
        qs = q_ref[:, pl.ds(col, PEER_KEYS)]
        scores = _dot_nt(keys_ref[p], qs)
        ts, ti = _top_rows(scores, PEER_TOPK)
        tops.append(ts)
        topi.append(ti)
    cand = jnp.concatenate([tops[0][a:a + 1, :] + tops[1] for a in range(PEER_TOPK)], axis=0)
    best, pos = _top_rows(cand, PEER_TOPK)
    pos_a = lax.shift_right_logical(pos, 4)
    pos_b = jnp.bitwise_and(pos, PEER_TOPK - 1)
    expert = _select_rows(topi[0], pos_a) * PEER_KEYS + _select_rows(topi[1], pos_b)
    e = jnp.exp(best - best[0:1, :])
    idx_ref[...] = expert * ROWS_PER_EXPERT
    gate_ref[...] = e / jnp.sum(e, axis=0, keepdims=True)


def _pack_table(tab):
    e, d = tab.shape
    tb = tab.astype(BF16)
    lo = lax.bitcast_convert_type(tb[:, : d // 2], jnp.uint16).astype(jnp.uint32)
    hi = lax.bitcast_convert_type(tb[:, d // 2:], jnp.uint16).astype(jnp.uint32)
    w = lo | (hi << 16)
    return lax.bitcast_convert_type(w, jnp.int32).reshape(e * ROWS_PER_EXPERT, LANES)


def _slot_mask():
    c = lax.broadcasted_iota(jnp.int32, (SUBLANES, 8 * N_SEL), 1)
    i = lax.broadcasted_iota(jnp.int32, (SUBLANES, 8 * N_SEL), 0)
    return i == 4 * (c % 2) + (c % 8) // 2


def _gather_rows(idx_ref, t, tab_ref, g_ref):
    for e in range(N_SEL):
        off = pl.multiple_of(idx_ref[t, e], ROWS_PER_EXPERT)
        g_ref[ROWS_PER_EXPERT * e: ROWS_PER_EXPERT * (e + 1), :] = tab_ref[pl.ds(off, ROWS_PER_EXPERT), :]


def _peer_act_kernel(idx_ref, h_ref, gate_ref, tab_ref, w_ref, g_ref, r_ref):
    mask = _slot_mask()

    def body(j, carry):
        for u in range(PEER_GROUP):
            t = j * PEER_GROUP + u
            _gather_rows(idx_ref, t, tab_ref, g_ref.at[u])
            r = _dot_nt(h_ref[t].astype(BF16), pltpu.bitcast(g_ref[u], BF16))
            r_ref[pl.ds(t, 1), :] = jnp.sum(jnp.where(mask, r, 0.0), axis=0, keepdims=True)
        return carry

    lax.fori_loop(0, PEER_T // PEER_GROUP, body, 0)
    c = lax.broadcasted_iota(jnp.int32, (8 * N_SEL, N_SEL), 0)
    e = lax.broadcasted_iota(jnp.int32, (8 * N_SEL, N_SEL), 1)
    gsel = (c // 8 == e).astype(F32)
    act = jnp.dot(r_ref[...], gsel, precision=lax.Precision.HIGHEST, preferred_element_type=F32)
    w_ref[...] = gate_ref[...] * _gelu(act)


def _peer_out_kernel(idx_ref, w_ref, tab_ref, o_ref, g_ref, wx_ref):
    mask = _slot_mask()
    e = lax.broadcasted_iota(jnp.int32, (N_SEL, 8 * N_SEL), 0)
    c = lax.broadcasted_iota(jnp.int32, (N_SEL, 8 * N_SEL), 1)
    expand = (c // 8 == e).astype(F32)
    wx_ref[...] = jnp.dot(w_ref[...], expand, precision=lax.Precision.HIGHEST, preferred_element_type=F32)

    def body(j, carry):
        for u in range(PEER_GROUP):
            t = j * PEER_GROUP + u
            _gather_rows(idx_ref, t, tab_ref, g_ref.at[u])
            wrow = jnp.broadcast_to(wx_ref[pl.ds(t, 1), :], (SUBLANES, 8 * N_SEL))
            w8 = jnp.where(mask, wrow, 0.0).astype(BF16)
            o_ref[t] = _dot(w8, pltpu.bitcast(g_ref[u], BF16))
        return carry

    lax.fori_loop(0, PEER_T // PEER_GROUP, body, 0)


def _final_ln_kernel(h_ref, f_ref, w_ref, b_ref, o_ref):
    o_ref[...] = _layer_norm(ALPHA * h_ref[...] + f_ref[...], w_ref[...], b_ref[...], LN_EPS)


def _const_spec(shape):
    zeros = (0,) * len(shape)
    return pl.BlockSpec(shape, lambda *_: zeros)


def _lane_bcast(v):
    return jnp.broadcast_to(v[:, :, None], v.shape + (LANES,)).astype(F32)


def kernel(x, w_in, sgu_ln_w, sgu_ln_b, sgu_w_s, sgu_b, ret_decay_fwd, ret_decay_bwd, ret_gn_w, w_out, ln1_w, ln1_b, peer_w_q, peer_sub_keys, peer_u, peer_v, ln2_w, ln2_b):
    b, s, d = x.shape
    n = b * s
    n_chunks = s // CHUNK
    assert d == 2 * GROUP_W and s % MIX_T == 0 and n % PEER_T == 0 and n % ROUTE_T == 0
    x2 = x.reshape(n, d)
    row = lambda v: v.reshape(1, -1).astype(F32)

    half = HEAD_DIM // 2
    inv = ROPE_BASE ** (-jnp.arange(half, dtype=F32) / half)
    ang = jnp.arange(s, dtype=F32)[:, None] * inv[None, :]
    cos2 = jnp.concatenate([jnp.cos(ang), jnp.cos(ang)], axis=-1)
    sin2 = jnp.concatenate([-jnp.sin(ang), jnp.sin(ang)], axis=-1)
    lgf = jax.nn.log_sigmoid(ret_decay_fwd[0].astype(F32))
    lgb = jax.nn.log_sigmoid(ret_decay_bwd[0].astype(F32))
    pos = jnp.arange(CHUNK, dtype=F32)
    diff = pos[:, None] - pos[None, :]
    decay = jnp.where(diff[None] >= 0, jnp.exp(jnp.maximum(diff, 0.0)[None] * lgf[:, None, None]),
                      jnp.exp(jnp.maximum(-diff, 0.0)[None] * lgb[:, None, None]))
    kwf = _lane_bcast(jnp.exp((CHUNK - 1 - pos)[None, :] * lgf[:, None]))
    kwb = _lane_bcast(jnp.exp(pos[None, :] * lgb[:, None]))
    qwf = _lane_bcast(jnp.exp((pos + 1.0)[None, :] * lgf[:, None]))
    qwb = _lane_bcast(jnp.exp((CHUNK - pos)[None, :] * lgb[:, None]))
    gcf = jnp.broadcast_to(jnp.exp(CHUNK * lgf)[:, None, None], (N_HEADS, HEAD_DIM, HEAD_DIM)).astype(F32)
    gcb = jnp.broadcast_to(jnp.exp(CHUNK * lgb)[:, None, None], (N_HEADS, HEAD_DIM, HEAD_DIM)).astype(F32)

    mix_grid = (b, s // MIX_T)
    tok_spec = lambda w: pl.BlockSpec((MIX_T, w), lambda i, j: (i * (s // MIX_T) + j, 0))
    pos_spec = pl.BlockSpec((MIX_T, HEAD_DIM), lambda i, j: (j, 0))
    state_spec = pl.BlockSpec((None, MIX_T // CHUNK, N_HEADS, HEAD_DIM, HEAD_DIM), lambda i, j: (i, j, 0, 0, 0))
    head_tab = _const_spec((N_HEADS, CHUNK, HEAD_DIM))
    mix_params = pltpu.CompilerParams(dimension_semantics=("arbitrary", "arbitrary"), vmem_limit_bytes=VMEM_LIMIT)
    state_shape = jax.ShapeDtypeStruct((b, n_chunks, N_HEADS, HEAD_DIM, HEAD_DIM), F32)

    a_out, r_pack, s_f, s_b = pl.pallas_call(
        _mix_front_kernel,
        grid=mix_grid,
        in_specs=[tok_spec(d), _const_spec(w_in.shape[1:]), _const_spec((1, GROUP_W)), _const_spec((1, GROUP_W)),
                  head_tab, head_tab, pos_spec, pos_spec, head_tab, head_tab],
        out_specs=[tok_spec(GROUP_W), tok_spec(4 * GROUP_W), state_spec, state_spec],
        out_shape=[jax.ShapeDtypeStruct((n, GROUP_W), BF16), jax.ShapeDtypeStruct((n, 4 * GROUP_W), BF16),
                   state_shape, state_shape],
        compiler_params=mix_params,
        name="mix_front",
    )(x2, w_in[0].astype(BF16), row(sgu_ln_w[0]), row(sgu_ln_b[0]), sgu_w_s[0].astype(BF16),
      _lane_bcast(sgu_b[0]), cos2, sin2, kwf, kwb)

    scan_state = pl.BlockSpec((None, n_chunks, None, HEAD_DIM, HEAD_DIM), lambda i, j: (i, 0, j, 0, 0))
    scan_tab = pl.BlockSpec((None, HEAD_DIM, HEAD_DIM), lambda i, j: (j, 0, 0))
    p_f, p_b = pl.pallas_call(
        _ret_scan_kernel,
        grid=(b, N_HEADS),
        in_specs=[scan_state, scan_state, scan_tab, scan_tab],
        out_specs=[scan_state, scan_state],
        out_shape=[state_shape, state_shape],
        compiler_params=mix_params,
        name="ret_scan",
    )(s_f, s_b, gcf, gcb)

    h = pl.pallas_call(
        _mix_back_kernel,
        grid=mix_grid,
        in_specs=[tok_spec(d), tok_spec(GROUP_W), tok_spec(4 * GROUP_W), state_spec, state_spec,
                  head_tab, head_tab, head_tab, _const_spec((1, GROUP_W)), _const_spec(w_out.shape[1:]),
                  _const_spec((1, d)), _const_spec((1, d))],
        out_specs=tok_spec(d),
        out_shape=jax.ShapeDtypeStruct((n, d), F32),
        scratch_shapes=[pltpu.VMEM((MIX_T, d), BF16)],
        compiler_params=mix_params,
        name="mix_back",
    )(x2, a_out, r_pack, p_f, p_b, decay.astype(F32), qwf, qwb, row(ret_gn_w[0]), w_out[0].astype(BF16),
      row(ln1_w[0]), row(ln1_b[0]))

    q_width = peer_w_q.shape[-1]
    idx_t, gate_t = pl.pallas_call(
        _peer_route_kernel,
        grid=(n // ROUTE_T, PEER_HEADS),
        in_specs=[pl.BlockSpec((ROUTE_T, d), lambda i, j: (i, 0)),
                  _const_spec((d, q_width)),
                  pl.BlockSpec((None, 2, PEER_KEYS, PEER_KEYS), lambda i, j: (j, 0, 0, 0))],
        out_specs=[pl.BlockSpec((PEER_TOPK, ROUTE_T), lambda i, j: (j, i)),
                   pl.BlockSpec((PEER_TOPK, ROUTE_T), lambda i, j: (j, i))],
        out_shape=[jax.ShapeDtypeStruct((N_SEL, n), jnp.int32), jax.ShapeDtypeStruct((N_SEL, n), F32)],
        scratch_shapes=[pltpu.VMEM((ROUTE_T, q_width), BF16)],
        compiler_params=mix_params,
        name="peer_route",
    )(h, peer_w_q[0].astype(BF16), peer_sub_keys[0].astype(BF16))
    idx4 = idx_t.T
    gates = gate_t.T

    steps = n // PEER_T
    u_pack = _pack_table(peer_u[0])
    v_pack = _pack_table(peer_v[0])
    tab_spec = pl.BlockSpec(u_pack.shape, lambda i: (0, 0), pipeline_mode=pl.Buffered(1))
    idx_spec = pl.BlockSpec((PEER_T, N_SEL), lambda i: (i, 0), memory_space=pltpu.SMEM)
    sel_spec = pl.BlockSpec((PEER_T, N_SEL), lambda i: (i, 0))
    tok3_spec = pl.BlockSpec((PEER_T, SUBLANES, LANES), lambda i: (i, 0, 0))
    peer_params = pltpu.CompilerParams(dimension_semantics=("arbitrary",), vmem_limit_bytes=VMEM_LIMIT)
    peer_scratch = [pltpu.VMEM((PEER_GROUP, ROWS_PER_EXPERT * N_SEL, LANES), jnp.int32),
                    pltpu.VMEM((PEER_T, 8 * N_SEL), F32)]
    weights = pl.pallas_call(
        _peer_act_kernel,
        grid=(steps,),
        in_specs=[idx_spec, tok3_spec, sel_spec, tab_spec],
        out_specs=sel_spec,
        out_shape=jax.ShapeDtypeStruct((n, N_SEL), F32),
        scratch_shapes=peer_scratch,
        compiler_params=peer_params,
        name="peer_act",
    )(idx4, h.reshape(n, SUBLANES, LANES), gates, u_pack)
    ffn = pl.pallas_call(
        _peer_out_kernel,
        grid=(steps,),
        in_specs=[idx_spec, sel_spec, tab_spec],
        out_specs=tok3_spec,
        out_shape=jax.ShapeDtypeStruct((n, SUBLANES, LANES), F32),
        scratch_shapes=peer_scratch,
        compiler_params=peer_params,
        name="peer_out",
    )(idx4, weights, v_pack)

    out = pl.pallas_call(
        _final_ln_kernel,
        grid=(n // MIX_T,),
        in_specs=[pl.BlockSpec((MIX_T, d), lambda i: (i, 0)), pl.BlockSpec((MIX_T, d), lambda i: (i, 0)),
                  _const_spec((1, d)), _const_spec((1, d))],
        out_specs=pl.BlockSpec((MIX_T, d), lambda i: (i, 0)),
        out_shape=jax.ShapeDtypeStruct((n, d), F32),
        compiler_params=peer_params,
        name="final_ln",
    )(h, ffn.reshape(n, d), row(ln2_w[0]), row(ln2_b[0]))
    return out.reshape(b, s, d)
```

```python
import functools

import jax
import jax.numpy as jnp
from jax import lax
from jax.experimental import pallas as pl
from jax.experimental.pallas import tpu as pltpu

F32 = jnp.float32
BF16 = jnp.bfloat16

LANES = 128
SUBLANES = 8
CHUNK = 128
HEAD_DIM = 128
N_HEADS = 4
GROUP_W = N_HEADS * HEAD_DIM
ROPE_BASE = 10000.0
LN_EPS = 1e-5
GN_EPS = 1e-5
ALPHA = 2.0 ** 0.25

PEER_HEADS = 8
PEER_KEYS = 128
PEER_TOPK = 16
N_SEL = PEER_HEADS * PEER_TOPK
ROWS_PER_EXPERT = 4
PEER_T = 128
PEER_GROUP = 16
MIX_T = 256
ROUTE_T = 256
VMEM_LIMIT = 56 * 1024 * 1024


def _gelu(x):
    return 0.5 * x * (1.0 + lax.erf(x * (2.0 ** -0.5)))


def _layer_norm(x, w, b, eps):
    mu = jnp.mean(x, axis=-1, keepdims=True)
    xc = x - mu
    var = jnp.mean(xc * xc, axis=-1, keepdims=True)
    return xc * lax.rsqrt(var + eps) * w + b


def _dot(a, b):
    return jnp.dot(a, b, preferred_element_type=F32)


def _dot_nt(a, b):
    return lax.dot_general(a, b, (((1,), (1,)), ((), ())), preferred_element_type=F32)


def _dot_tn(a, b):
    return lax.dot_general(a, b, (((0,), (0,)), ((), ())), preferred_element_type=F32)


def _mix_front_kernel(x_ref, w_in_ref, sln_w_ref, sln_b_ref, ws_ref, bs_ref, cos_ref, sin_ref,
                      kwf_ref, kwb_ref, a_ref, r_ref, sf_ref, sb_ref):
    xb = x_ref[...].astype(BF16)
    chunks = MIX_T // CHUNK

    u = _gelu(_dot(xb, w_in_ref[:, 0:GROUP_W]))
    v = _gelu(_dot(xb, w_in_ref[:, GROUP_W:2 * GROUP_W]))
    vn = _layer_norm(v, sln_w_ref[...], sln_b_ref[...], LN_EPS).astype(BF16)
    for c in range(chunks):
        rows = slice(c * CHUNK, (c + 1) * CHUNK)
        for h in range(N_HEADS):
            cols = slice(h * HEAD_DIM, (h + 1) * HEAD_DIM)
            mixed = _dot(ws_ref[h], vn[rows, cols]) + bs_ref[h]
            a_ref[rows, cols] = (u[rows, cols] * mixed).astype(a_ref.dtype)

    cos = cos_ref[...]
    sin = sin_ref[...]
    q = _dot(xb, w_in_ref[:, 2 * GROUP_W:3 * GROUP_W])
    k = _dot(xb, w_in_ref[:, 3 * GROUP_W:4 * GROUP_W])
    vr = _dot(xb, w_in_ref[:, 4 * GROUP_W:5 * GROUP_W])
    g = _dot(xb, w_in_ref[:, 5 * GROUP_W:6 * GROUP_W])
    r_ref[:, 2 * GROUP_W:3 * GROUP_W] = vr.astype(r_ref.dtype)
    r_ref[:, 3 * GROUP_W:4 * GROUP_W] = (g * jax.nn.sigmoid(g)).astype(r_ref.dtype)
    for h in range(N_HEADS):
        cols = slice(h * HEAD_DIM, (h + 1) * HEAD_DIM)
        qh = q[:, cols]
        kh = k[:, cols]
        qh = qh * cos + pltpu.roll(qh, HEAD_DIM // 2, 1) * sin
        kh = (kh * cos + pltpu.roll(kh, HEAD_DIM // 2, 1) * sin) * (HEAD_DIM ** -0.5)
        r_ref[:, cols] = qh.astype(r_ref.dtype)
        r_ref[:, GROUP_W + h * HEAD_DIM:GROUP_W + (h + 1) * HEAD_DIM] = kh.astype(r_ref.dtype)
        for c in range(chunks):
            rows = slice(c * CHUNK, (c + 1) * CHUNK)
            vc = vr[rows, cols].astype(BF16)
            sf_ref[c, h] = _dot_tn((kh[rows] * kwf_ref[h]).astype(BF16), vc)
            sb_ref[c, h] = _dot_tn((kh[rows] * kwb_ref[h]).astype(BF16), vc)


def _ret_scan_kernel(sf_ref, sb_ref, gf_ref, gb_ref, pf_ref, pb_ref):
    n_chunks = sf_ref.shape[0]
    gf = gf_ref[...]
    gb = gb_ref[...]

    def fwd(c, st):
        pf_ref[c] = st
        return gf * st + sf_ref[c]

    lax.fori_loop(0, n_chunks, fwd, jnp.zeros((HEAD_DIM, HEAD_DIM), F32))

    def bwd(i, st):
        c = n_chunks - 1 - i
        pb_ref[c] = st
        return gb * st + sb_ref[c]

    lax.fori_loop(0, n_chunks, bwd, jnp.zeros((HEAD_DIM, HEAD_DIM), F32))


def _mix_back_kernel(x_ref, a_ref, r_ref, pf_ref, pb_ref, dec_ref, qwf_ref, qwb_ref, gn_ref,
                     w_out_ref, ln_w_ref, ln_b_ref, h_ref, cat_ref):
    chunks = MIX_T // CHUNK
    cat_ref[:, 0:GROUP_W] = a_ref[...]
    for c in range(chunks):
        rows = slice(c * CHUNK, (c + 1) * CHUNK)
        for h in range(N_HEADS):
            cols = slice(h * HEAD_DIM, (h + 1) * HEAD_DIM)
            qh = r_ref[rows, cols]
            kh = r_ref[rows, GROUP_W + h * HEAD_DIM:GROUP_W + (h + 1) * HEAD_DIM]
            vh = r_ref[rows, 2 * GROUP_W + h * HEAD_DIM:2 * GROUP_W + (h + 1) * HEAD_DIM]
            gate = r_ref[rows, 3 * GROUP_W + h * HEAD_DIM:3 * GROUP_W + (h + 1) * HEAD_DIM].astype(F32)
            scores = _dot_nt(qh, kh) * dec_ref[h]
            o = _dot(scores.astype(BF16), vh)
            qf = qh.astype(F32)
            o = o + _dot((qf * qwf_ref[h]).astype(BF16), pf_ref[c, h].astype(BF16))
            o = o + _dot((qf * qwb_ref[h]).astype(BF16), pb_ref[c, h].astype(BF16))
            mu = jnp.mean(o, axis=-1, keepdims=True)
            oc = o - mu
            var = jnp.mean(oc * oc, axis=-1, keepdims=True)
            on = oc * lax.rsqrt(var + GN_EPS) * gn_ref[:, cols]
            cat_ref[rows, GROUP_W + h * HEAD_DIM:GROUP_W + (h + 1) * HEAD_DIM] = (gate * on).astype(cat_ref.dtype)
    mixed = _dot(cat_ref[...], w_out_ref[...])
    h_ref[...] = _layer_norm(ALPHA * x_ref[...] + mixed, ln_w_ref[...], ln_b_ref[...], LN_EPS)


def _top_rows(s, k):
    n_rows = s.shape[0]
    row = lax.broadcasted_iota(jnp.int32, s.shape, 0).astype(F32)
    vals, idxs = [], []
    for _ in range(k):
        m = jnp.max(s, axis=0, keepdims=True)
        i = jnp.min(jnp.where(s == m, row, float(n_rows)), axis=0, keepdims=True)
        vals.append(m)
        idxs.append(i)
        s = jnp.where(row == i, -jnp.inf, s)
    return jnp.concatenate(vals, axis=0), jnp.concatenate(idxs, axis=0).astype(jnp.int32)


def _select_rows(table, pos):
    out = jnp.zeros(pos.shape, table.dtype)
    for a in range(table.shape[0]):
        out = jnp.where(pos == a, table[a:a + 1, :], out)
    return out


def _pair_groups(k):
    groups = []
    a = 0
    while a < k and k // (a + 1) > 1:
        n_b = k // (a + 1)
        for b0 in range(0, n_b, SUBLANES):
            groups.append((a, 0, b0, 1, min(SUBLANES, n_b - b0)))
        a += 1
    for a0 in range(a, k, SUBLANES):
        groups.append((a0, 1, 0, 0, min(SUBLANES, k - a0)))
    return groups


def _route_tile(scores):
    tops, topi = zip(*(_top_rows(sc, PEER_TOPK) for sc in scores))
    sub = lax.broadcasted_iota(jnp.int32, (SUBLANES, LANES), 0)
    pieces = []
    for a0, a_step, b0, b_step, n_valid in _pair_groups(PEER_TOPK):
        sa = tops[0][a0:a0 + SUBLANES] if a_step else tops[0][a0:a0 + 1]
        sb = tops[1][b0:b0 + SUBLANES] if b_step else tops[1][b0:b0 + 1]
        piece = sa + sb
        pieces.append(piece if n_valid == SUBLANES else jnp.where(sub < n_valid, piece, -jnp.inf))
    best, pos = _top_rows(jnp.concatenate(pieces, axis=0), PEER_TOPK)
    grp = lax.shift_right_logical(pos, 3)
    lane_row = jnp.bitwise_and(pos, SUBLANES - 1)
    pos_a = jnp.zeros_like(pos)
    pos_b = jnp.zeros_like(pos)
    for gi, (a0, a_step, b0, b_step, _) in enumerate(_pair_groups(PEER_TOPK)):
        pos_a = jnp.where(grp == gi, a0 + a_step * lane_row, pos_a)
        pos_b = jnp.where(grp == gi, b0 + b_step * lane_row, pos_b)
    expert = _select_rows(topi[0], pos_a) * PEER_KEYS + _select_rows(topi[1], pos_b)
    e = jnp.exp(best - best[0:1, :])
    return expert * ROWS_PER_EXPERT, e / jnp.sum(e, axis=0, keepdims=True)


def _peer_route_kernel(h_ref, wq_ref, keys_ref, idx_ref, gate_ref, q_ref):
    head = pl.program_id(1)

    @pl.when(head == 0)
    def _():
        q_ref[...] = _dot(h_ref[...].astype(BF16), wq_ref[...]).astype(q_ref.dtype)

    scores = []
    for p in range(2):
        col = pl.multiple_of((head * 2 + p) * PEER_KEYS, PEER_KEYS)
        scores.append(_dot_nt(keys_ref[p], q_ref[:, pl.ds(col, PEER_KEYS)]))
    for j in range(ROUTE_T // LANES):
        cols = slice(j * LANES, (j + 1) * LANES)
        idx_ref[:, cols], gate_ref[:, cols] = _route_tile([sc[:, cols] for sc in scores])


def _pack_table(tab):
    e, d = tab.shape
    tb = tab.astype(BF16)
    lo = lax.bitcast_convert_type(tb[:, : d // 2], jnp.uint16).astype(jnp.uint32)
    hi = lax.bitcast_convert_type(tb[:, d // 2:], jnp.uint16).astype(jnp.uint32)
    w = lo | (hi << 16)
    return lax.bitcast_convert_type(w, jnp.int32).reshape(e * ROWS_PER_EXPERT, LANES)


def _slot_mask():
    c = lax.broadcasted_iota(jnp.int32, (SUBLANES, 8 * N_SEL), 1)
    i = lax.broadcasted_iota(jnp.int32, (SUBLANES, 8 * N_SEL), 0)
    return i == 4 * (c % 2) + (c % 8) // 2


def _gather_rows(idx_s, slot, u, tab_ref, g_ref):
    for e in range(N_SEL):
        off = pl.multiple_of(idx_s[slot, u, e], ROWS_PER_EXPERT)
        g_ref[ROWS_PER_EXPERT * e: ROWS_PER_EXPERT * (e + 1), :] = tab_ref[pl.ds(off, ROWS_PER_EXPERT), :]


def _idx_copy(idx_hbm, idx_s, sem, group, slot):
    rows = idx_hbm.at[pl.ds(group * PEER_GROUP, PEER_GROUP)]
    return pltpu.make_async_copy(rows, idx_s.at[slot], sem.at[slot])


def _group_loop(idx_hbm, idx_s, sem, process):
    step = pl.program_id(0)
    groups = PEER_T // PEER_GROUP
    first = step * groups
    total = pl.num_programs(0) * groups

    @pl.when(step == 0)
    def _():
        _idx_copy(idx_hbm, idx_s, sem, 0, 0).start()

    def pair(jj, carry):
        ga = first + 2 * jj
        _idx_copy(idx_hbm, idx_s, sem, ga + 1, 1).start()
        _idx_copy(idx_hbm, idx_s, sem, ga, 0).wait()
        process(2 * jj * PEER_GROUP, 0)

        @pl.when(ga + 2 < total)
        def _():
            _idx_copy(idx_hbm, idx_s, sem, ga + 2, 0).start()

        _idx_copy(idx_hbm, idx_s, sem, ga + 1, 1).wait()
        process((2 * jj + 1) * PEER_GROUP, 1)
        return carry

    lax.fori_loop(0, groups // 2, pair, 0)


def _peer_act_kernel(idx_hbm, h_ref, gate_ref, tab_ref, w_ref, g_ref, r_ref, idx_s, sem):
    mask = _slot_mask()

    def process(t0, slot):
        for u in range(PEER_GROUP):
            t = t0 + u
            _gather_rows(idx_s, slot, u, tab_ref, g_ref.at[u])
            r = _dot_nt(h_ref[t].astype(BF16), pltpu.bitcast(g_ref[u], BF16))
            r_ref[pl.ds(t, 1), :] = jnp.sum(jnp.where(mask, r, 0.0), axis=0, keepdims=True)

    _group_loop(idx_hbm, idx_s, sem, process)
    c = lax.broadcasted_iota(jnp.int32, (8 * N_SEL, N_SEL), 0)
    e = lax.broadcasted_iota(jnp.int32, (8 * N_SEL, N_SEL), 1)
    gsel = (c // 8 == e).astype(F32)
    act = jnp.dot(r_ref[...], gsel, precision=lax.Precision.HIGHEST, preferred_element_type=F32)
    w_ref[...] = gate_ref[...] * _gelu(act)


def _peer_out_kernel(idx_hbm, w_ref, tab_ref, o_ref, g_ref, wx_ref, idx_s, sem):
    mask = _slot_mask()
    e = lax.broadcasted_iota(jnp.int32, (N_SEL, 8 * N_SEL), 0)
    c = lax.broadcasted_iota(jnp.int32, (N_SEL, 8 * N_SEL), 1)
    expand = (c // 8 == e).astype(F32)
    wx_ref[...] = jnp.dot(w_ref[...], expand, precision=lax.Precision.HIGHEST, preferred_element_type=F32)

    def process(t0, slot):
        for u in range(PEER_GROUP):
            t = t0 + u
            _gather_rows(idx_s, slot, u, tab_ref, g_ref.at[u])
            wrow = jnp.broadcast_to(wx_ref[pl.ds(t, 1), :], (SUBLANES, 8 * N_SEL))
            w8 = jnp.where(mask, wrow, 0.0).astype(BF16)
            o_ref[t] = _dot(w8, pltpu.bitcast(g_ref[u], BF16))

    _group_loop(idx_hbm, idx_s, sem, process)


def _final_ln_kernel(h_ref, f_ref, w_ref, b_ref, o_ref):
    o_ref[...] = _layer_norm(ALPHA * h_ref[...] + f_ref[...], w_ref[...], b_ref[...], LN_EPS)


def _const_spec(shape):
    zeros = (0,) * len(shape)
    return pl.BlockSpec(shape, lambda *_: zeros)


def _lane_bcast(v):
    return jnp.broadcast_to(v[:, :, None], v.shape + (LANES,)).astype(F32)


def kernel(x, w_in, sgu_ln_w, sgu_ln_b, sgu_w_s, sgu_b, ret_decay_fwd, ret_decay_bwd, ret_gn_w, w_out, ln1_w, ln1_b, peer_w_q, peer_sub_keys, peer_u, peer_v, ln2_w, ln2_b):
    b, s, d = x.shape
    n = b * s
    n_chunks = s // CHUNK
    assert d == 2 * GROUP_W and s % MIX_T == 0 and n % PEER_T == 0 and n % ROUTE_T == 0
    x2 = x.reshape(n, d)
    row = lambda v: v.reshape(1, -1).astype(F32)

    half = HEAD_DIM // 2
    inv = ROPE_BASE ** (-jnp.arange(half, dtype=F32) / half)
    ang = jnp.arange(s, dtype=F32)[:, None] * inv[None, :]
    cos2 = jnp.concatenate([jnp.cos(ang), jnp.cos(ang)], axis=-1)
    sin2 = jnp.concatenate([-jnp.sin(ang), jnp.sin(ang)], axis=-1)
    lgf = jax.nn.log_sigmoid(ret_decay_fwd[0].astype(F32))
    lgb = jax.nn.log_sigmoid(ret_decay_bwd[0].astype(F32))
    pos = jnp.arange(CHUNK, dtype=F32)
    diff = pos[:, None] - pos[None, :]
    decay = jnp.where(diff[None] >= 0, jnp.exp(jnp.maximum(diff, 0.0)[None] * lgf[:, None, None]),
                      jnp.exp(jnp.maximum(-diff, 0.0)[None] * lgb[:, None, None]))
    kwf = _lane_bcast(jnp.exp((CHUNK - 1 - pos)[None, :] * lgf[:, None]))
    kwb = _lane_bcast(jnp.exp(pos[None, :] * lgb[:, None]))
    qwf = _lane_bcast(jnp.exp((pos + 1.0)[None, :] * lgf[:, None]))
    qwb = _lane_bcast(jnp.exp((CHUNK - pos)[None, :] * lgb[:, None]))
    gcf = jnp.broadcast_to(jnp.exp(CHUNK * lgf)[:, None, None], (N_HEADS, HEAD_DIM, HEAD_DIM)).astype(F32)
    gcb = jnp.broadcast_to(jnp.exp(CHUNK * lgb)[:, None, None], (N_HEADS, HEAD_DIM, HEAD_DIM)).astype(F32)

    mix_grid = (b, s // MIX_T)
    tok_spec = lambda w: pl.BlockSpec((MIX_T, w), lambda i, j: (i * (s // MIX_T) + j, 0))
    pos_spec = pl.BlockSpec((MIX_T, HEAD_DIM), lambda i, j: (j, 0))
    state_spec = pl.BlockSpec((None, MIX_T // CHUNK, N_HEADS, HEAD_DIM, HEAD_DIM), lambda i, j: (i, j, 0, 0, 0))
    head_tab = _const_spec((N_HEADS, CHUNK, HEAD_DIM))
    mix_params = pltpu.CompilerParams(dimension_semantics=("arbitrary", "arbitrary"), vmem_limit_bytes=VMEM_LIMIT)
    state_shape = jax.ShapeDtypeStruct((b, n_chunks, N_HEADS, HEAD_DIM, HEAD_DIM), F32)

    a_out, r_pack, s_f, s_b = pl.pallas_call(
        _mix_front_kernel,
        grid=mix_grid,
        in_specs=[tok_spec(d), _const_spec(w_in.shape[1:]), _const_spec((1, GROUP_W)), _const_spec((1, GROUP_W)),
                  head_tab, head_tab, pos_spec, pos_spec, head_tab, head_tab],
        out_specs=[tok_spec(GROUP_W), tok_spec(4 * GROUP_W), state_spec, state_spec],
        out_shape=[jax.ShapeDtypeStruct((n, GROUP_W), BF16), jax.ShapeDtypeStruct((n, 4 * GROUP_W), BF16),
                   state_shape, state_shape],
        compiler_params=mix_params,
        name="mix_front",
    )(x2, w_in[0].astype(BF16), row(sgu_ln_w[0]), row(sgu_ln_b[0]), sgu_w_s[0].astype(BF16),
      _lane_bcast(sgu_b[0]), cos2, sin2, kwf, kwb)

    scan_state = pl.BlockSpec((None, n_chunks, None, HEAD_DIM, HEAD_DIM), lambda i, j: (i, 0, j, 0, 0))
    scan_tab = pl.BlockSpec((None, HEAD_DIM, HEAD_DIM), lambda i, j: (j, 0, 0))
    p_f, p_b = pl.pallas_call(
        _ret_scan_kernel,
        grid=(b, N_HEADS),
        in_specs=[scan_state, scan_state, scan_tab, scan_tab],
        out_specs=[scan_state, scan_state],
        out_shape=[state_shape, state_shape],
        compiler_params=mix_params,
        name="ret_scan",
    )(s_f, s_b, gcf, gcb)

    h = pl.pallas_call(
        _mix_back_kernel,
        grid=mix_grid,
        in_specs=[tok_spec(d), tok_spec(GROUP_W), tok_spec(4 * GROUP_W), state_spec, state_spec,
                  head_tab, head_tab, head_tab, _const_spec((1, GROUP_W)), _const_spec(w_out.shape[1:]),
                  _const_spec((1, d)), _const_spec((1, d))],
        out_specs=tok_spec(d),
        out_shape=jax.ShapeDtypeStruct((n, d), F32),
        scratch_shapes=[pltpu.VMEM((MIX_T, d), BF16)],
        compiler_params=mix_params,
        name="mix_back",
    )(x2, a_out, r_pack, p_f, p_b, decay.astype(F32), qwf, qwb, row(ret_gn_w[0]), w_out[0].astype(BF16),
      row(ln1_w[0]), row(ln1_b[0]))

    q_width = peer_w_q.shape[-1]
    idx_t, gate_t = pl.pallas_call(
        _peer_route_kernel,
        grid=(n // ROUTE_T, PEER_HEADS),
        in_specs=[pl.BlockSpec((ROUTE_T, d), lambda i, j: (i, 0)),
                  _const_spec((d, q_width)),
                  pl.BlockSpec((None, 2, PEER_KEYS, PEER_KEYS), lambda i, j: (j, 0, 0, 0))],
        out_specs=[pl.BlockSpec((PEER_TOPK, ROUTE_T), lambda i, j: (j, i)),
                   pl.BlockSpec((PEER_TOPK, ROUTE_T), lambda i, j: (j, i))],
        out_shape=[jax.ShapeDtypeStruct((N_SEL, n), jnp.int32), jax.ShapeDtypeStruct((N_SEL, n), F32)],
        scratch_shapes=[pltpu.VMEM((ROUTE_T, q_width), BF16)],
        compiler_params=mix_params,
        name="peer_route",
    )(h, peer_w_q[0].astype(BF16), peer_sub_keys[0].astype(BF16))
    idx4 = idx_t.T
    gates = gate_t.T

    steps = n // PEER_T
    u_pack = _pack_table(peer_u[0])
    v_pack = _pack_table(peer_v[0])
    tab_spec = pl.BlockSpec(u_pack.shape, lambda i: (0, 0), pipeline_mode=pl.Buffered(1))
    idx_spec = pl.BlockSpec(memory_space=pl.ANY)
    sel_spec = pl.BlockSpec((PEER_T, N_SEL), lambda i: (i, 0))
    tok3_spec = pl.BlockSpec((PEER_T, SUBLANES, LANES), lambda i: (i, 0, 0))
    peer_params = pltpu.CompilerParams(dimension_semantics=("arbitrary",), vmem_limit_bytes=VMEM_LIMIT)
    peer_scratch = [pltpu.VMEM((PEER_GROUP, ROWS_PER_EXPERT * N_SEL, LANES), jnp.int32),
                    pltpu.VMEM((PEER_T, 8 * N_SEL), F32),
                    pltpu.SMEM((2, PEER_GROUP, N_SEL), jnp.int32),
                    pltpu.SemaphoreType.DMA((2,))]
    weights = pl.pallas_call(
        _peer_act_kernel,
        grid=(steps,),
        in_specs=[idx_spec, tok3_spec, sel_spec, tab_spec],
        out_specs=sel_spec,
        out_shape=jax.ShapeDtypeStruct((n, N_SEL), F32),
        scratch_shapes=peer_scratch,
        compiler_params=peer_params,
        name="peer_act",
    )(idx4, h.reshape(n, SUBLANES, LANES), gates, u_pack)
    ffn = pl.pallas_call(
        _peer_out_kernel,
        grid=(steps,),
        in_specs=[idx_spec, sel_spec, tab_spec],
        out_specs=tok3_spec,
        out_shape=jax.ShapeDtypeStruct((n, SUBLANES, LANES), F32),
        scratch_shapes=peer_scratch,
        compiler_params=peer_params,
        name="peer_out",
    )(idx4, weights, v_pack)

    out = pl.pallas_call(
        _final_ln_kernel,
        grid=(n // MIX_T,),
        in_specs=[pl.BlockSpec((MIX_T, d), lambda i: (i, 0)), pl.BlockSpec((MIX_T, d), lambda i: (i, 0)),
                  _const_spec((1, d)), _const_spec((1, d))],
        out_specs=pl.BlockSpec((MIX_T, d), lambda i: (i, 0)),
        out_shape=jax.ShapeDtypeStruct((n, d), F32),
        compiler_params=peer_params,
        name="final_ln",
    )(h, ffn.reshape(n, d), row(ln2_w[0]), row(ln2_b[0]))
    return out.reshape(b, s, d)
```

```python
import functools

import jax
import jax.numpy as jnp
from jax import lax
from jax.experimental import pallas as pl
from jax.experimental.pallas import tpu as pltpu

F32 = jnp.float32
BF16 = jnp.bfloat16

LANES = 128
SUBLANES = 8
CHUNK = 128
HEAD_DIM = 128
N_HEADS = 4
GROUP_W = N_HEADS * HEAD_DIM
ROPE_BASE = 10000.0
LN_EPS = 1e-5
GN_EPS = 1e-5
ALPHA = 2.0 ** 0.25

PEER_HEADS = 8
PEER_KEYS = 128
PEER_TOPK = 16
N_SEL = PEER_HEADS * PEER_TOPK
ROWS_PER_EXPERT = 8
PEER_T = 128
PEER_GROUP = 16
MIX_T = 256
ROUTE_T = 256
VMEM_LIMIT = 56 * 1024 * 1024


def _gelu(x):
    return 0.5 * x * (1.0 + lax.erf(x * (2.0 ** -0.5)))


def _layer_norm(x, w, b, eps):
    mu = jnp.mean(x, axis=-1, keepdims=True)
    xc = x - mu
    var = jnp.mean(xc * xc, axis=-1, keepdims=True)
    return xc * lax.rsqrt(var + eps) * w + b


def _dot(a, b):
    return jnp.dot(a, b, preferred_element_type=F32)


def _dot_nt(a, b):
    return lax.dot_general(a, b, (((1,), (1,)), ((), ())), preferred_element_type=F32)


def _dot_tn(a, b):
    return lax.dot_general(a, b, (((0,), (0,)), ((), ())), preferred_element_type=F32)


def _mix_front_kernel(x_ref, w_in_ref, sln_w_ref, sln_b_ref, ws_ref, bs_ref, cos_ref, sin_ref,
                      kwf_ref, kwb_ref, a_ref, r_ref, sf_ref, sb_ref):
    xb = x_ref[...].astype(BF16)
    chunks = MIX_T // CHUNK

    u = _gelu(_dot(xb, w_in_ref[:, 0:GROUP_W]))
    v = _gelu(_dot(xb, w_in_ref[:, GROUP_W:2 * GROUP_W]))
    vn = _layer_norm(v, sln_w_ref[...], sln_b_ref[...], LN_EPS).astype(BF16)
    for c in range(chunks):
        rows = slice(c * CHUNK, (c + 1) * CHUNK)
        for h in range(N_HEADS):
            cols = slice(h * HEAD_DIM, (h + 1) * HEAD_DIM)
            mixed = _dot(ws_ref[h], vn[rows, cols]) + bs_ref[h]
            a_ref[rows, cols] = (u[rows, cols] * mixed).astype(a_ref.dtype)

    cos = cos_ref[...]
    sin = sin_ref[...]
    q = _dot(xb, w_in_ref[:, 2 * GROUP_W:3 * GROUP_W])
    k = _dot(xb, w_in_ref[:, 3 * GROUP_W:4 * GROUP_W])
    vr = _dot(xb, w_in_ref[:, 4 * GROUP_W:5 * GROUP_W])
    g = _dot(xb, w_in_ref[:, 5 * GROUP_W:6 * GROUP_W])
    r_ref[:, 2 * GROUP_W:3 * GROUP_W] = vr.astype(r_ref.dtype)
    r_ref[:, 3 * GROUP_W:4 * GROUP_W] = (g * jax.nn.sigmoid(g)).astype(r_ref.dtype)
    for h in range(N_HEADS):
        cols = slice(h * HEAD_DIM, (h + 1) * HEAD_DIM)
        qh = q[:, cols]
        kh = k[:, cols]
        qh = qh * cos + pltpu.roll(qh, HEAD_DIM // 2, 1) * sin
        kh = (kh * cos + pltpu.roll(kh, HEAD_DIM // 2, 1) * sin) * (HEAD_DIM ** -0.5)
        r_ref[:, cols] = qh.astype(r_ref.dtype)
        r_ref[:, GROUP_W + h * HEAD_DIM:GROUP_W + (h + 1) * HEAD_DIM] = kh.astype(r_ref.dtype)
        for c in range(chunks):
            rows = slice(c * CHUNK, (c + 1) * CHUNK)
            vc = vr[rows, cols].astype(BF16)
            sf_ref[c, h] = _dot_tn((kh[rows] * kwf_ref[h]).astype(BF16), vc)
            sb_ref[c, h] = _dot_tn((kh[rows] * kwb_ref[h]).astype(BF16), vc)


def _ret_scan_kernel(sf_ref, sb_ref, gf_ref, gb_ref, pf_ref, pb_ref):
    n_chunks = sf_ref.shape[0]
    gf = gf_ref[...]
    gb = gb_ref[...]

    def fwd(c, st):
        pf_ref[c] = st
        return gf * st + sf_ref[c]

    lax.fori_loop(0, n_chunks, fwd, jnp.zeros((HEAD_DIM, HEAD_DIM), F32))

    def bwd(i, st):
        c = n_chunks - 1 - i
        pb_ref[c] = st
        return gb * st + sb_ref[c]

    lax.fori_loop(0, n_chunks, bwd, jnp.zeros((HEAD_DIM, HEAD_DIM), F32))


def _mix_back_kernel(x_ref, a_ref, r_ref, pf_ref, pb_ref, dec_ref, qwf_ref, qwb_ref, gn_ref,
                     w_out_ref, ln_w_ref, ln_b_ref, h_ref, cat_ref):
    chunks = MIX_T // CHUNK
    cat_ref[:, 0:GROUP_W] = a_ref[...]
    for c in range(chunks):
        rows = slice(c * CHUNK, (c + 1) * CHUNK)
        for h in range(N_HEADS):
            cols = slice(h * HEAD_DIM, (h + 1) * HEAD_DIM)
            qh = r_ref[rows, cols]
            kh = r_ref[rows, GROUP_W + h * HEAD_DIM:GROUP_W + (h + 1) * HEAD_DIM]
            vh = r_ref[rows, 2 * GROUP_W + h * HEAD_DIM:2 * GROUP_W + (h + 1) * HEAD_DIM]
            gate = r_ref[rows, 3 * GROUP_W + h * HEAD_DIM:3 * GROUP_W + (h + 1) * HEAD_DIM].astype(F32)
            scores = _dot_nt(qh, kh) * dec_ref[h]
            o = _dot(scores.astype(BF16), vh)
            qf = qh.astype(F32)
            o = o + _dot((qf * qwf_ref[h]).astype(BF16), pf_ref[c, h].astype(BF16))
            o = o + _dot((qf * qwb_ref[h]).astype(BF16), pb_ref[c, h].astype(BF16))
            mu = jnp.mean(o, axis=-1, keepdims=True)
            oc = o - mu
            var = jnp.mean(oc * oc, axis=-1, keepdims=True)
            on = oc * lax.rsqrt(var + GN_EPS) * gn_ref[:, cols]
            cat_ref[rows, GROUP_W + h * HEAD_DIM:GROUP_W + (h + 1) * HEAD_DIM] = (gate * on).astype(cat_ref.dtype)
    mixed = _dot(cat_ref[...], w_out_ref[...])
    h_ref[...] = _layer_norm(ALPHA * x_ref[...] + mixed, ln_w_ref[...], ln_b_ref[...], LN_EPS)


def _top_rows(s, k):
    n_rows = s.shape[0]
    row = lax.broadcasted_iota(jnp.int32, s.shape, 0).astype(F32)
    vals, idxs = [], []
    for _ in range(k):
        m = jnp.max(s, axis=0, keepdims=True)
        i = jnp.min(jnp.where(s == m, row, float(n_rows)), axis=0, keepdims=True)
        vals.append(m)
        idxs.append(i)
        s = jnp.where(row == i, -jnp.inf, s)
    return jnp.concatenate(vals, axis=0), jnp.concatenate(idxs, axis=0).astype(jnp.int32)


def _select_rows(table, pos):
    out = jnp.zeros(pos.shape, table.dtype)
    for a in range(table.shape[0]):
        out = jnp.where(pos == a, table[a:a + 1, :], out)
    return out


def _pair_groups(k):
    groups = []
    a = 0
    while a < k and k // (a + 1) > 1:
        n_b = k // (a + 1)
        for b0 in range(0, n_b, SUBLANES):
            groups.append((a, 0, b0, 1, min(SUBLANES, n_b - b0)))
        a += 1
    for a0 in range(a, k, SUBLANES):
        groups.append((a0, 1, 0, 0, min(SUBLANES, k - a0)))
    return groups


def _route_tile(scores):
    tops, topi = zip(*(_top_rows(sc, PEER_TOPK) for sc in scores))
    sub = lax.broadcasted_iota(jnp.int32, (SUBLANES, LANES), 0)
    pieces = []
    for a0, a_step, b0, b_step, n_valid in _pair_groups(PEER_TOPK):
        sa = tops[0][a0:a0 + SUBLANES] if a_step else tops[0][a0:a0 + 1]
        sb = tops[1][b0:b0 + SUBLANES] if b_step else tops[1][b0:b0 + 1]
        piece = sa + sb
        pieces.append(piece if n_valid == SUBLANES else jnp.where(sub < n_valid, piece, -jnp.inf))
    best, pos = _top_rows(jnp.concatenate(pieces, axis=0), PEER_TOPK)
    grp = lax.shift_right_logical(pos, 3)
    lane_row = jnp.bitwise_and(pos, SUBLANES - 1)
    pos_a = jnp.zeros_like(pos)
    pos_b = jnp.zeros_like(pos)
    for gi, (a0, a_step, b0, b_step, _) in enumerate(_pair_groups(PEER_TOPK)):
        pos_a = jnp.where(grp == gi, a0 + a_step * lane_row, pos_a)
        pos_b = jnp.where(grp == gi, b0 + b_step * lane_row, pos_b)
    expert = _select_rows(topi[0], pos_a) * PEER_KEYS + _select_rows(topi[1], pos_b)
    e = jnp.exp(best - best[0:1, :])
    return expert, e / jnp.sum(e, axis=0, keepdims=True)


def _peer_route_kernel(h_ref, wq_ref, keys_ref, idx_ref, gate_ref, q_ref):
    head = pl.program_id(1)

    @pl.when(head == 0)
    def _():
        q_ref[...] = _dot(h_ref[...].astype(BF16), wq_ref[...]).astype(q_ref.dtype)

    scores = []
    for p in range(2):
        col = pl.multiple_of((head * 2 + p) * PEER_KEYS, PEER_KEYS)
        scores.append(_dot_nt(keys_ref[p], q_ref[:, pl.ds(col, PEER_KEYS)]))
    for j in range(ROUTE_T // LANES):
        cols = slice(j * LANES, (j + 1) * LANES)
        idx_ref[:, cols], gate_ref[:, cols] = _route_tile([sc[:, cols] for sc in scores])


def _pack_table(tab):
    e, d = tab.shape
    return tab.astype(BF16).reshape(e, ROWS_PER_EXPERT, LANES)


def _slot_mask():
    c = lax.broadcasted_iota(jnp.int32, (SUBLANES, ROWS_PER_EXPERT * N_SEL), 1)
    i = lax.broadcasted_iota(jnp.int32, (SUBLANES, ROWS_PER_EXPERT * N_SEL), 0)
    return i == c % ROWS_PER_EXPERT


def _gather_rows(idx_s, slot, u, tab_ref):
    pieces = []
    for e in range(N_SEL):
        pieces.append(tab_ref[idx_s[slot, u, e]])
    return jnp.concatenate(pieces, axis=0)


def _idx_copy(idx_hbm, idx_s, sem, group, slot):
    rows = idx_hbm.at[pl.ds(group * PEER_GROUP, PEER_GROUP)]
    return pltpu.make_async_copy(rows, idx_s.at[slot], sem.at[slot])


def _group_loop(idx_hbm, idx_s, sem, process):
    step = pl.program_id(0)
    groups = PEER_T // PEER_GROUP
    first = step * groups
    total = pl.num_programs(0) * groups

    @pl.when(step == 0)
    def _():
        _idx_copy(idx_hbm, idx_s, sem, 0, 0).start()

    def pair(jj, carry):
        ga = first + 2 * jj
        _idx_copy(idx_hbm, idx_s, sem, ga + 1, 1).start()
        _idx_copy(idx_hbm, idx_s, sem, ga, 0).wait()
        process(2 * jj * PEER_GROUP, 0)

        @pl.when(ga + 2 < total)
        def _():
            _idx_copy(idx_hbm, idx_s, sem, ga + 2, 0).start()

        _idx_copy(idx_hbm, idx_s, sem, ga + 1, 1).wait()
        process((2 * jj + 1) * PEER_GROUP, 1)
        return carry

    lax.fori_loop(0, groups // 2, pair, 0)


def _peer_act_kernel(idx_hbm, h_ref, gate_ref, tab_ref, w_ref, r_ref, idx_s, sem):
    mask = _slot_mask()

    def process(t0, slot):
        for u in range(PEER_GROUP):
            t = t0 + u
            rows = _gather_rows(idx_s, slot, u, tab_ref)
            r = _dot_nt(h_ref[t].astype(BF16), rows)
            r_ref[pl.ds(t, 1), :] = jnp.sum(jnp.where(mask, r, 0.0), axis=0, keepdims=True)

    _group_loop(idx_hbm, idx_s, sem, process)
    c = lax.broadcasted_iota(jnp.int32, (8 * N_SEL, N_SEL), 0)
    e = lax.broadcasted_iota(jnp.int32, (8 * N_SEL, N_SEL), 1)
    gsel = (c // 8 == e).astype(F32)
    act = jnp.dot(r_ref[...], gsel, precision=lax.Precision.HIGHEST, preferred_element_type=F32)
    w_ref[...] = gate_ref[...] * _gelu(act)


def _peer_out_kernel(idx_hbm, w_ref, tab_ref, o_ref, wx_ref, idx_s, sem):
    mask = _slot_mask()
    e = lax.broadcasted_iota(jnp.int32, (N_SEL, 8 * N_SEL), 0)
    c = lax.broadcasted_iota(jnp.int32, (N_SEL, 8 * N_SEL), 1)
    expand = (c // 8 == e).astype(F32)
    wx_ref[...] = jnp.dot(w_ref[...], expand, precision=lax.Precision.HIGHEST, preferred_element_type=F32)

    def process(t0, slot):
        for u in range(PEER_GROUP):
            t = t0 + u
            rows = _gather_rows(idx_s, slot, u, tab_ref)
            wrow = jnp.broadcast_to(wx_ref[pl.ds(t, 1), :], (SUBLANES, 8 * N_SEL))
            w8 = jnp.where(mask, wrow, 0.0).astype(BF16)
            o_ref[t] = _dot(w8, rows)

    _group_loop(idx_hbm, idx_s, sem, process)


def _final_ln_kernel(h_ref, f_ref, w_ref, b_ref, o_ref):
    o_ref[...] = _layer_norm(ALPHA * h_ref[...] + f_ref[...], w_ref[...], b_ref[...], LN_EPS)


def _const_spec(shape):
    zeros = (0,) * len(shape)
    return pl.BlockSpec(shape, lambda *_: zeros)


def _lane_bcast(v):
    return jnp.broadcast_to(v[:, :, None], v.shape + (LANES,)).astype(F32)


def kernel(x, w_in, sgu_ln_w, sgu_ln_b, sgu_w_s, sgu_b, ret_decay_fwd, ret_decay_bwd, ret_gn_w, w_out, ln1_w, ln1_b, peer_w_q, peer_sub_keys, peer_u, peer_v, ln2_w, ln2_b):
    b, s, d = x.shape
    n = b * s
    n_chunks = s // CHUNK
    assert d == 2 * GROUP_W and s % MIX_T == 0 and n % PEER_T == 0 and n % ROUTE_T == 0
    x2 = x.reshape(n, d)
    row = lambda v: v.reshape(1, -1).astype(F32)

    half = HEAD_DIM // 2
    inv = ROPE_BASE ** (-jnp.arange(half, dtype=F32) / half)
    ang = jnp.arange(s, dtype=F32)[:, None] * inv[None, :]
    cos2 = jnp.concatenate([jnp.cos(ang), jnp.cos(ang)], axis=-1)
    sin2 = jnp.concatenate([-jnp.sin(ang), jnp.sin(ang)], axis=-1)
    lgf = jax.nn.log_sigmoid(ret_decay_fwd[0].astype(F32))
    lgb = jax.nn.log_sigmoid(ret_decay_bwd[0].astype(F32))
    pos = jnp.arange(CHUNK, dtype=F32)
    diff = pos[:, None] - pos[None, :]
    decay = jnp.where(diff[None] >= 0, jnp.exp(jnp.maximum(diff, 0.0)[None] * lgf[:, None, None]),
                      jnp.exp(jnp.maximum(-diff, 0.0)[None] * lgb[:, None, None]))
    kwf = _lane_bcast(jnp.exp((CHUNK - 1 - pos)[None, :] * lgf[:, None]))
    kwb = _lane_bcast(jnp.exp(pos[None, :] * lgb[:, None]))
    qwf = _lane_bcast(jnp.exp((pos + 1.0)[None, :] * lgf[:, None]))
    qwb = _lane_bcast(jnp.exp((CHUNK - pos)[None, :] * lgb[:, None]))
    gcf = jnp.broadcast_to(jnp.exp(CHUNK * lgf)[:, None, None], (N_HEADS, HEAD_DIM, HEAD_DIM)).astype(F32)
    gcb = jnp.broadcast_to(jnp.exp(CHUNK * lgb)[:, None, None], (N_HEADS, HEAD_DIM, HEAD_DIM)).astype(F32)

    mix_grid = (b, s // MIX_T)
    tok_spec = lambda w: pl.BlockSpec((MIX_T, w), lambda i, j: (i * (s // MIX_T) + j, 0))
    pos_spec = pl.BlockSpec((MIX_T, HEAD_DIM), lambda i, j: (j, 0))
    state_spec = pl.BlockSpec((None, MIX_T // CHUNK, N_HEADS, HEAD_DIM, HEAD_DIM), lambda i, j: (i, j, 0, 0, 0))
    head_tab = _const_spec((N_HEADS, CHUNK, HEAD_DIM))
    mix_params = pltpu.CompilerParams(dimension_semantics=("arbitrary", "arbitrary"), vmem_limit_bytes=VMEM_LIMIT)
    state_shape = jax.ShapeDtypeStruct((b, n_chunks, N_HEADS, HEAD_DIM, HEAD_DIM), F32)

    a_out, r_pack, s_f, s_b = pl.pallas_call(
        _mix_front_kernel,
        grid=mix_grid,
        in_specs=[tok_spec(d), _const_spec(w_in.shape[1:]), _const_spec((1, GROUP_W)), _const_spec((1, GROUP_W)),
                  head_tab, head_tab, pos_spec, pos_spec, head_tab, head_tab],
        out_specs=[tok_spec(GROUP_W), tok_spec(4 * GROUP_W), state_spec, state_spec],
        out_shape=[jax.ShapeDtypeStruct((n, GROUP_W), BF16), jax.ShapeDtypeStruct((n, 4 * GROUP_W), BF16),
                   state_shape, state_shape],
        compiler_params=mix_params,
        name="mix_front",
    )(x2, w_in[0].astype(BF16), row(sgu_ln_w[0]), row(sgu_ln_b[0]), sgu_w_s[0].astype(BF16),
      _lane_bcast(sgu_b[0]), cos2, sin2, kwf, kwb)

    scan_state = pl.BlockSpec((None, n_chunks, None, HEAD_DIM, HEAD_DIM), lambda i, j: (i, 0, j, 0, 0))
    scan_tab = pl.BlockSpec((None, HEAD_DIM, HEAD_DIM), lambda i, j: (j, 0, 0))
    p_f, p_b = pl.pallas_call(
        _ret_scan_kernel,
        grid=(b, N_HEADS),
        in_specs=[scan_state, scan_state, scan_tab, scan_tab],
        out_specs=[scan_state, scan_state],
        out_shape=[state_shape, state_shape],
        compiler_params=mix_params,
        name="ret_scan",
    )(s_f, s_b, gcf, gcb)

    h = pl.pallas_call(
        _mix_back_kernel,
        grid=mix_grid,
        in_specs=[tok_spec(d), tok_spec(GROUP_W), tok_spec(4 * GROUP_W), state_spec, state_spec,
                  head_tab, head_tab, head_tab, _const_spec((1, GROUP_W)), _const_spec(w_out.shape[1:]),
                  _const_spec((1, d)), _const_spec((1, d))],
        out_specs=tok_spec(d),
        out_shape=jax.ShapeDtypeStruct((n, d), F32),
        scratch_shapes=[pltpu.VMEM((MIX_T, d), BF16)],
        compiler_params=mix_params,
        name="mix_back",
    )(x2, a_out, r_pack, p_f, p_b, decay.astype(F32), qwf, qwb, row(ret_gn_w[0]), w_out[0].astype(BF16),
      row(ln1_w[0]), row(ln1_b[0]))

    q_width = peer_w_q.shape[-1]
    idx_t, gate_t = pl.pallas_call(
        _peer_route_kernel,
        grid=(n // ROUTE_T, PEER_HEADS),
        in_specs=[pl.BlockSpec((ROUTE_T, d), lambda i, j: (i, 0)),
                  _const_spec((d, q_width)),
                  pl.BlockSpec((None, 2, PEER_KEYS, PEER_KEYS), lambda i, j: (j, 0, 0, 0))],
        out_specs=[pl.BlockSpec((PEER_TOPK, ROUTE_T), lambda i, j: (j, i)),
                   pl.BlockSpec((PEER_TOPK, ROUTE_T), lambda i, j: (j, i))],
        out_shape=[jax.ShapeDtypeStruct((N_SEL, n), jnp.int32), jax.ShapeDtypeStruct((N_SEL, n), F32)],
        scratch_shapes=[pltpu.VMEM((ROUTE_T, q_width), BF16)],
        compiler_params=mix_params,
        name="peer_route",
    )(h, peer_w_q[0].astype(BF16), peer_sub_keys[0].astype(BF16))
    idx4 = idx_t.T
    gates = gate_t.T

    steps = n // PEER_T
    u_pack = _pack_table(peer_u[0])
    v_pack = _pack_table(peer_v[0])
    tab_spec = pl.BlockSpec(u_pack.shape, lambda i: (0, 0, 0), pipeline_mode=pl.Buffered(1))
    idx_spec = pl.BlockSpec(memory_space=pl.ANY)
    sel_spec = pl.BlockSpec((PEER_T, N_SEL), lambda i: (i, 0))
    tok3_spec = pl.BlockSpec((PEER_T, SUBLANES, LANES), lambda i: (i, 0, 0))
    peer_params = pltpu.CompilerParams(dimension_semantics=("arbitrary",), vmem_limit_bytes=VMEM_LIMIT)
    peer_scratch = [pltpu.VMEM((PEER_T, 8 * N_SEL), F32),
                    pltpu.SMEM((2, PEER_GROUP, N_SEL), jnp.int32),
                    pltpu.SemaphoreType.DMA((2,))]
    weights = pl.pallas_call(
        _peer_act_kernel,
        grid=(steps,),
        in_specs=[idx_spec, tok3_spec, sel_spec, tab_spec],
        out_specs=sel_spec,
        out_shape=jax.ShapeDtypeStruct((n, N_SEL), F32),
        scratch_shapes=peer_scratch,
        compiler_params=peer_params,
        name="peer_act",
    )(idx4, h.reshape(n, SUBLANES, LANES), gates, u_pack)
    ffn = pl.pallas_call(
        _peer_out_kernel,
        grid=(steps,),
        in_specs=[idx_spec, sel_spec, tab_spec],
        out_specs=tok3_spec,
        out_shape=jax.ShapeDtypeStruct((n, SUBLANES, LANES), F32),
        scratch_shapes=peer_scratch,
        compiler_params=peer_params,
        name="peer_out",
    )(idx4, weights, v_pack)

    out = pl.pallas_call(
        _final_ln_kernel,
        grid=(n // MIX_T,),
        in_specs=[pl.BlockSpec((MIX_T, d), lambda i: (i, 0)), pl.BlockSpec((MIX_T, d), lambda i: (i, 0)),
                  _const_spec((1, d)), _const_spec((1, d))],
        out_specs=pl.BlockSpec((MIX_T, d), lambda i: (i, 0)),
        out_shape=jax.ShapeDtypeStruct((n, d), F32),
        compiler_params=peer_params,
        name="final_ln",
    )(h, ffn.reshape(n, d), row(ln2_w[0]), row(ln2_b[0]))
    return out.reshape(b, s, d)
```

```python
import functools

import jax
import jax.numpy as jnp
from jax import lax
from jax.experimental import pallas as pl
from jax.experimental.pallas import tpu as pltpu

F32 = jnp.float32
BF16 = jnp.bfloat16

LANES = 128
SUBLANES = 8
CHUNK = 128
HEAD_DIM = 128
N_HEADS = 4
GROUP_W = N_HEADS * HEAD_DIM
ROPE_BASE = 10000.0
LN_EPS = 1e-5
GN_EPS = 1e-5
ALPHA = 2.0 ** 0.25

PEER_HEADS = 8
PEER_KEYS = 128
PEER_TOPK = 16
N_SEL = PEER_HEADS * PEER_TOPK
ROWS_PER_EXPERT = 8
PEER_T = 128
PEER_GROUP = 32
MIX_T = 256
ROUTE_T = 256
ROUTE_TILE = 256
VMEM_LIMIT = 56 * 1024 * 1024


def _gelu(x):
    return 0.5 * x * (1.0 + lax.erf(x * (2.0 ** -0.5)))


def _layer_norm(x, w, b, eps):
    mu = jnp.mean(x, axis=-1, keepdims=True)
    xc = x - mu
    var = jnp.mean(xc * xc, axis=-1, keepdims=True)
    return xc * lax.rsqrt(var + eps) * w + b


def _dot(a, b):
    return jnp.dot(a, b, preferred_element_type=F32)


def _dot_nt(a, b):
    return lax.dot_general(a, b, (((1,), (1,)), ((), ())), preferred_element_type=F32)


def _dot_tn(a, b):
    return lax.dot_general(a, b, (((0,), (0,)), ((), ())), preferred_element_type=F32)


def _mix_front_kernel(x_ref, w_in_ref, sln_w_ref, sln_b_ref, ws_ref, bs_ref, cos_ref, sin_ref,
                      kwf_ref, kwb_ref, a_ref, r_ref, sf_ref, sb_ref):
    xb = x_ref[...].astype(BF16)
    chunks = MIX_T // CHUNK

    u = _gelu(_dot(xb, w_in_ref[:, 0:GROUP_W]))
    v = _gelu(_dot(xb, w_in_ref[:, GROUP_W:2 * GROUP_W]))
    vn = _layer_norm(v, sln_w_ref[...], sln_b_ref[...], LN_EPS).astype(BF16)
    for c in range(chunks):
        rows = slice(c * CHUNK, (c + 1) * CHUNK)
        for h in range(N_HEADS):
            cols = slice(h * HEAD_DIM, (h + 1) * HEAD_DIM)
            mixed = _dot(ws_ref[h], vn[rows, cols]) + bs_ref[h]
            a_ref[rows, cols] = (u[rows, cols] * mixed).astype(a_ref.dtype)

    cos = cos_ref[...]
    sin = sin_ref[...]
    q = _dot(xb, w_in_ref[:, 2 * GROUP_W:3 * GROUP_W])
    k = _dot(xb, w_in_ref[:, 3 * GROUP_W:4 * GROUP_W])
    vr = _dot(xb, w_in_ref[:, 4 * GROUP_W:5 * GROUP_W])
    g = _dot(xb, w_in_ref[:, 5 * GROUP_W:6 * GROUP_W])
    r_ref[:, 2 * GROUP_W:3 * GROUP_W] = vr.astype(r_ref.dtype)
    r_ref[:, 3 * GROUP_W:4 * GROUP_W] = (g * jax.nn.sigmoid(g)).astype(r_ref.dtype)
    for h in range(N_HEADS):
        cols = slice(h * HEAD_DIM, (h + 1) * HEAD_DIM)
        qh = q[:, cols]
        kh = k[:, cols]
        qh = qh * cos + pltpu.roll(qh, HEAD_DIM // 2, 1) * sin
        kh = (kh * cos + pltpu.roll(kh, HEAD_DIM // 2, 1) * sin) * (HEAD_DIM ** -0.5)
        r_ref[:, cols] = qh.astype(r_ref.dtype)
        r_ref[:, GROUP_W + h * HEAD_DIM:GROUP_W + (h + 1) * HEAD_DIM] = kh.astype(r_ref.dtype)
        for c in range(chunks):
            rows = slice(c * CHUNK, (c + 1) * CHUNK)
            vc = vr[rows, cols].astype(BF16)
            sf_ref[c, h] = _dot_tn((kh[rows] * kwf_ref[h]).astype(BF16), vc)
            sb_ref[c, h] = _dot_tn((kh[rows] * kwb_ref[h]).astype(BF16), vc)


def _ret_scan_kernel(sf_ref, sb_ref, gf_ref, gb_ref, pf_ref, pb_ref):
    n_chunks = sf_ref.shape[0]
    gf = gf_ref[...]
    gb = gb_ref[...]

    def fwd(c, st):
        pf_ref[c] = st
        return gf * st + sf_ref[c]

    lax.fori_loop(0, n_chunks, fwd, jnp.zeros((HEAD_DIM, HEAD_DIM), F32))

    def bwd(i, st):
        c = n_chunks - 1 - i
        pb_ref[c] = st
        return gb * st + sb_ref[c]

    lax.fori_loop(0, n_chunks, bwd, jnp.zeros((HEAD_DIM, HEAD_DIM), F32))


def _mix_back_kernel(x_ref, a_ref, r_ref, pf_ref, pb_ref, dec_ref, qwf_ref, qwb_ref, gn_ref,
                     w_out_ref, ln_w_ref, ln_b_ref, h_ref, cat_ref):
    chunks = MIX_T // CHUNK
    cat_ref[:, 0:GROUP_W] = a_ref[...]
    for c in range(chunks):
        rows = slice(c * CHUNK, (c + 1) * CHUNK)
        for h in range(N_HEADS):
            cols = slice(h * HEAD_DIM, (h + 1) * HEAD_DIM)
            qh = r_ref[rows, cols]
            kh = r_ref[rows, GROUP_W + h * HEAD_DIM:GROUP_W + (h + 1) * HEAD_DIM]
            vh = r_ref[rows, 2 * GROUP_W + h * HEAD_DIM:2 * GROUP_W + (h + 1) * HEAD_DIM]
            gate = r_ref[rows, 3 * GROUP_W + h * HEAD_DIM:3 * GROUP_W + (h + 1) * HEAD_DIM].astype(F32)
            scores = _dot_nt(qh, kh) * dec_ref[h]
            o = _dot(scores.astype(BF16), vh)
            qf = qh.astype(F32)
            o = o + _dot((qf * qwf_ref[h]).astype(BF16), pf_ref[c, h].astype(BF16))
            o = o + _dot((qf * qwb_ref[h]).astype(BF16), pb_ref[c, h].astype(BF16))
            mu = jnp.mean(o, axis=-1, keepdims=True)
            oc = o - mu
            var = jnp.mean(oc * oc, axis=-1, keepdims=True)
            on = oc * lax.rsqrt(var + GN_EPS) * gn_ref[:, cols]
            cat_ref[rows, GROUP_W + h * HEAD_DIM:GROUP_W + (h + 1) * HEAD_DIM] = (gate * on).astype(cat_ref.dtype)
    mixed = _dot(cat_ref[...], w_out_ref[...])
    h_ref[...] = _layer_norm(ALPHA * x_ref[...] + mixed, ln_w_ref[...], ln_b_ref[...], LN_EPS)


def _top_rows(s, k):
    n_rows = s.shape[0]
    row = lax.broadcasted_iota(jnp.int32, s.shape, 0).astype(F32)
    vals, idxs = [], []
    for _ in range(k):
        m = jnp.max(s, axis=0, keepdims=True)
        i = jnp.min(jnp.where(s == m, row, float(n_rows)), axis=0, keepdims=True)
        vals.append(m)
        idxs.append(i)
        s = jnp.where(row == i, -jnp.inf, s)
    return jnp.concatenate(vals, axis=0), jnp.concatenate(idxs, axis=0).astype(jnp.int32)


def _select_rows(table, pos):
    out = jnp.zeros(pos.shape, table.dtype)
    for a in range(table.shape[0]):
        out = jnp.where(pos == a, table[a:a + 1, :], out)
    return out


def _pair_groups(k):
    groups = []
    a = 0
    while a < k and k // (a + 1) > 1:
        n_b = k // (a + 1)
        for b0 in range(0, n_b, SUBLANES):
            groups.append((a, 0, b0, 1, min(SUBLANES, n_b - b0)))
        a += 1
    for a0 in range(a, k, SUBLANES):
        groups.append((a0, 1, 0, 0, min(SUBLANES, k - a0)))
    return groups


def _route_tile(scores):
    tops, topi = zip(*(_top_rows(sc, PEER_TOPK) for sc in scores))
    sub = lax.broadcasted_iota(jnp.int32, (SUBLANES, scores[0].shape[1]), 0)
    pieces = []
    for a0, a_step, b0, b_step, n_valid in _pair_groups(PEER_TOPK):
        sa = tops[0][a0:a0 + SUBLANES] if a_step else tops[0][a0:a0 + 1]
        sb = tops[1][b0:b0 + SUBLANES] if b_step else tops[1][b0:b0 + 1]
        piece = sa + sb
        pieces.append(piece if n_valid == SUBLANES else jnp.where(sub < n_valid, piece, -jnp.inf))
    best, pos = _top_rows(jnp.concatenate(pieces, axis=0), PEER_TOPK)
    grp = lax.shift_right_logical(pos, 3)
    lane_row = jnp.bitwise_and(pos, SUBLANES - 1)
    pos_a = jnp.zeros_like(pos)
    pos_b = jnp.zeros_like(pos)
    for gi, (a0, a_step, b0, b_step, _) in enumerate(_pair_groups(PEER_TOPK)):
        pos_a = jnp.where(grp == gi, a0 + a_step * lane_row, pos_a)
        pos_b = jnp.where(grp == gi, b0 + b_step * lane_row, pos_b)
    expert = _select_rows(topi[0], pos_a) * PEER_KEYS + _select_rows(topi[1], pos_b)
    e = jnp.exp(best - best[0:1, :])
    return expert, e / jnp.sum(e, axis=0, keepdims=True)


def _peer_route_kernel(h_ref, wq_ref, keys_ref, idx_ref, gate_ref, q_ref):
    head = pl.program_id(1)

    @pl.when(head == 0)
    def _():
        q_ref[...] = _dot(h_ref[...].astype(BF16), wq_ref[...]).astype(q_ref.dtype)

    scores = []
    for p in range(2):
        col = pl.multiple_of((head * 2 + p) * PEER_KEYS, PEER_KEYS)
        scores.append(_dot_nt(keys_ref[p], q_ref[:, pl.ds(col, PEER_KEYS)]))
    for j in range(ROUTE_T // ROUTE_TILE):
        cols = slice(j * ROUTE_TILE, (j + 1) * ROUTE_TILE)
        idx_ref[:, cols], gate_ref[:, cols] = _route_tile([sc[:, cols] for sc in scores])


def _pack_table(tab):
    e, d = tab.shape
    return tab.astype(BF16).reshape(e, ROWS_PER_EXPERT, LANES)


def _slot_mask():
    c = lax.broadcasted_iota(jnp.int32, (SUBLANES, ROWS_PER_EXPERT * N_SEL), 1)
    i = lax.broadcasted_iota(jnp.int32, (SUBLANES, ROWS_PER_EXPERT * N_SEL), 0)
    return i == c % ROWS_PER_EXPERT


def _gather_rows(idx_s, slot, u, tab_ref):
    pieces = []
    for e in range(N_SEL):
        pieces.append(tab_ref[idx_s[slot, u, e]])
    return jnp.concatenate(pieces, axis=0)


def _idx_copy(idx_hbm, idx_s, sem, group, slot):
    rows = idx_hbm.at[pl.ds(group * PEER_GROUP, PEER_GROUP)]
    return pltpu.make_async_copy(rows, idx_s.at[slot], sem.at[slot])


def _group_loop(idx_hbm, idx_s, sem, process):
    step = pl.program_id(0)
    groups = PEER_T // PEER_GROUP
    first = step * groups
    total = pl.num_programs(0) * groups

    @pl.when(step == 0)
    def _():
        _idx_copy(idx_hbm, idx_s, sem, 0, 0).start()

    def pair(jj, carry):
        ga = first + 2 * jj
        _idx_copy(idx_hbm, idx_s, sem, ga + 1, 1).start()
        _idx_copy(idx_hbm, idx_s, sem, ga, 0).wait()
        process(2 * jj * PEER_GROUP, 0)

        @pl.when(ga + 2 < total)
        def _():
            _idx_copy(idx_hbm, idx_s, sem, ga + 2, 0).start()

        _idx_copy(idx_hbm, idx_s, sem, ga + 1, 1).wait()
        process((2 * jj + 1) * PEER_GROUP, 1)
        return carry

    lax.fori_loop(0, groups // 2, pair, 0)


def _peer_act_kernel(idx_hbm, h_ref, gate_ref, tab_ref, w_ref, r_ref, idx_s, sem):
    mask = _slot_mask()

    def process(t0, slot):
        for u in range(PEER_GROUP):
            t = t0 + u
            rows = _gather_rows(idx_s, slot, u, tab_ref)
            r = _dot_nt(h_ref[t].astype(BF16), rows)
            r_ref[pl.ds(t, 1), :] = jnp.sum(jnp.where(mask, r, 0.0), axis=0, keepdims=True)

    _group_loop(idx_hbm, idx_s, sem, process)
    c = lax.broadcasted_iota(jnp.int32, (8 * N_SEL, N_SEL), 0)
    e = lax.broadcasted_iota(jnp.int32, (8 * N_SEL, N_SEL), 1)
    gsel = (c // 8 == e).astype(F32)
    act = jnp.dot(r_ref[...], gsel, precision=lax.Precision.HIGHEST, preferred_element_type=F32)
    w_ref[...] = gate_ref[...] * _gelu(act)


def _peer_out_kernel(idx_hbm, w_ref, tab_ref, o_ref, wx_ref, idx_s, sem):
    mask = _slot_mask()
    e = lax.broadcasted_iota(jnp.int32, (N_SEL, 8 * N_SEL), 0)
    c = lax.broadcasted_iota(jnp.int32, (N_SEL, 8 * N_SEL), 1)
    expand = (c // 8 == e).astype(F32)
    wx_ref[...] = jnp.dot(w_ref[...], expand, precision=lax.Precision.HIGHEST, preferred_element_type=F32)

    def process(t0, slot):
        for u in range(PEER_GROUP):
            t = t0 + u
            rows = _gather_rows(idx_s, slot, u, tab_ref)
            wrow = jnp.broadcast_to(wx_ref[pl.ds(t, 1), :], (SUBLANES, 8 * N_SEL))
            w8 = jnp.where(mask, wrow, 0.0).astype(BF16)
            o_ref[t] = _dot(w8, rows)

    _group_loop(idx_hbm, idx_s, sem, process)


def _final_ln_kernel(h_ref, f_ref, w_ref, b_ref, o_ref):
    o_ref[...] = _layer_norm(ALPHA * h_ref[...] + f_ref[...], w_ref[...], b_ref[...], LN_EPS)


def _const_spec(shape):
    zeros = (0,) * len(shape)
    return pl.BlockSpec(shape, lambda *_: zeros)


def _lane_bcast(v):
    return jnp.broadcast_to(v[:, :, None], v.shape + (LANES,)).astype(F32)


def kernel(x, w_in, sgu_ln_w, sgu_ln_b, sgu_w_s, sgu_b, ret_decay_fwd, ret_decay_bwd, ret_gn_w, w_out, ln1_w, ln1_b, peer_w_q, peer_sub_keys, peer_u, peer_v, ln2_w, ln2_b):
    b, s, d = x.shape
    n = b * s
    n_chunks = s // CHUNK
    assert d == 2 * GROUP_W and s % MIX_T == 0 and n % PEER_T == 0 and n % ROUTE_T == 0
    x2 = x.reshape(n, d)
    row = lambda v: v.reshape(1, -1).astype(F32)

    half = HEAD_DIM // 2
    inv = ROPE_BASE ** (-jnp.arange(half, dtype=F32) / half)
    ang = jnp.arange(s, dtype=F32)[:, None] * inv[None, :]
    cos2 = jnp.concatenate([jnp.cos(ang), jnp.cos(ang)], axis=-1)
    sin2 = jnp.concatenate([-jnp.sin(ang), jnp.sin(ang)], axis=-1)
    lgf = jax.nn.log_sigmoid(ret_decay_fwd[0].astype(F32))
    lgb = jax.nn.log_sigmoid(ret_decay_bwd[0].astype(F32))
    pos = jnp.arange(CHUNK, dtype=F32)
    diff = pos[:, None] - pos[None, :]
    decay = jnp.where(diff[None] >= 0, jnp.exp(jnp.maximum(diff, 0.0)[None] * lgf[:, None, None]),
                      jnp.exp(jnp.maximum(-diff, 0.0)[None] * lgb[:, None, None]))
    kwf = _lane_bcast(jnp.exp((CHUNK - 1 - pos)[None, :] * lgf[:, None]))
    kwb = _lane_bcast(jnp.exp(pos[None, :] * lgb[:, None]))
    qwf = _lane_bcast(jnp.exp((pos + 1.0)[None, :] * lgf[:, None]))
    qwb = _lane_bcast(jnp.exp((CHUNK - pos)[None, :] * lgb[:, None]))
    gcf = jnp.broadcast_to(jnp.exp(CHUNK * lgf)[:, None, None], (N_HEADS, HEAD_DIM, HEAD_DIM)).astype(F32)
    gcb = jnp.broadcast_to(jnp.exp(CHUNK * lgb)[:, None, None], (N_HEADS, HEAD_DIM, HEAD_DIM)).astype(F32)

    mix_grid = (b, s // MIX_T)
    tok_spec = lambda w: pl.BlockSpec((MIX_T, w), lambda i, j: (i * (s // MIX_T) + j, 0))
    pos_spec = pl.BlockSpec((MIX_T, HEAD_DIM), lambda i, j: (j, 0))
    state_spec = pl.BlockSpec((None, MIX_T // CHUNK, N_HEADS, HEAD_DIM, HEAD_DIM), lambda i, j: (i, j, 0, 0, 0))
    head_tab = _const_spec((N_HEADS, CHUNK, HEAD_DIM))
    mix_params = pltpu.CompilerParams(dimension_semantics=("arbitrary", "arbitrary"), vmem_limit_bytes=VMEM_LIMIT)
    state_shape = jax.ShapeDtypeStruct((b, n_chunks, N_HEADS, HEAD_DIM, HEAD_DIM), F32)

    a_out, r_pack, s_f, s_b = pl.pallas_call(
        _mix_front_kernel,
        grid=mix_grid,
        in_specs=[tok_spec(d), _const_spec(w_in.shape[1:]), _const_spec((1, GROUP_W)), _const_spec((1, GROUP_W)),
                  head_tab, head_tab, pos_spec, pos_spec, head_tab, head_tab],
        out_specs=[tok_spec(GROUP_W), tok_spec(4 * GROUP_W), state_spec, state_spec],
        out_shape=[jax.ShapeDtypeStruct((n, GROUP_W), BF16), jax.ShapeDtypeStruct((n, 4 * GROUP_W), BF16),
                   state_shape, state_shape],
        compiler_params=mix_params,
        name="mix_front",
    )(x2, w_in[0].astype(BF16), row(sgu_ln_w[0]), row(sgu_ln_b[0]), sgu_w_s[0].astype(BF16),
      _lane_bcast(sgu_b[0]), cos2, sin2, kwf, kwb)

    scan_state = pl.BlockSpec((None, n_chunks, None, HEAD_DIM, HEAD_DIM), lambda i, j: (i, 0, j, 0, 0))
    scan_tab = pl.BlockSpec((None, HEAD_DIM, HEAD_DIM), lambda i, j: (j, 0, 0))
    p_f, p_b = pl.pallas_call(
        _ret_scan_kernel,
        grid=(b, N_HEADS),
        in_specs=[scan_state, scan_state, scan_tab, scan_tab],
        out_specs=[scan_state, scan_state],
        out_shape=[state_shape, state_shape],
        compiler_params=mix_params,
        name="ret_scan",
    )(s_f, s_b, gcf, gcb)

    h = pl.pallas_call(
        _mix_back_kernel,
        grid=mix_grid,
        in_specs=[tok_spec(d), tok_spec(GROUP_W), tok_spec(4 * GROUP_W), state_spec, state_spec,
                  head_tab, head_tab, head_tab, _const_spec((1, GROUP_W)), _const_spec(w_out.shape[1:]),
                  _const_spec((1, d)), _const_spec((1, d))],
        out_specs=tok_spec(d),
        out_shape=jax.ShapeDtypeStruct((n, d), F32),
        scratch_shapes=[pltpu.VMEM((MIX_T, d), BF16)],
        compiler_params=mix_params,
        name="mix_back",
    )(x2, a_out, r_pack, p_f, p_b, decay.astype(F32), qwf, qwb, row(ret_gn_w[0]), w_out[0].astype(BF16),
      row(ln1_w[0]), row(ln1_b[0]))

    q_width = peer_w_q.shape[-1]
    idx_t, gate_t = pl.pallas_call(
        _peer_route_kernel,
        grid=(n // ROUTE_T, PEER_HEADS),
        in_specs=[pl.BlockSpec((ROUTE_T, d), lambda i, j: (i, 0)),
                  _const_spec((d, q_width)),
                  pl.BlockSpec((None, 2, PEER_KEYS, PEER_KEYS), lambda i, j: (j, 0, 0, 0))],
        out_specs=[pl.BlockSpec((PEER_TOPK, ROUTE_T), lambda i, j: (j, i)),
                   pl.BlockSpec((PEER_TOPK, ROUTE_T), lambda i, j: (j, i))],
        out_shape=[jax.ShapeDtypeStruct((N_SEL, n), jnp.int32), jax.ShapeDtypeStruct((N_SEL, n), F32)],
        scratch_shapes=[pltpu.VMEM((ROUTE_T, q_width), BF16)],
        compiler_params=mix_params,
        name="peer_route",
    )(h, peer_w_q[0].astype(BF16), peer_sub_keys[0].astype(BF16))
    idx4 = idx_t.T
    gates = gate_t.T

    steps = n // PEER_T
    u_pack = _pack_table(peer_u[0])
    v_pack = _pack_table(peer_v[0])
    tab_spec = pl.BlockSpec(u_pack.shape, lambda i: (0, 0, 0), pipeline_mode=pl.Buffered(1))
    idx_spec = pl.BlockSpec(memory_space=pl.ANY)
    sel_spec = pl.BlockSpec((PEER_T, N_SEL), lambda i: (i, 0))
    tok3_spec = pl.BlockSpec((PEER_T, SUBLANES, LANES), lambda i: (i, 0, 0))
    peer_params = pltpu.CompilerParams(dimension_semantics=("arbitrary",), vmem_limit_bytes=VMEM_LIMIT)
    peer_scratch = [pltpu.VMEM((PEER_T, 8 * N_SEL), F32),
                    pltpu.SMEM((2, PEER_GROUP, N_SEL), jnp.int32),
                    pltpu.SemaphoreType.DMA((2,))]
    weights = pl.pallas_call(
        _peer_act_kernel,
        grid=(steps,),
        in_specs=[idx_spec, tok3_spec, sel_spec, tab_spec],
        out_specs=sel_spec,
        out_shape=jax.ShapeDtypeStruct((n, N_SEL), F32),
        scratch_shapes=peer_scratch,
        compiler_params=peer_params,
        name="peer_act",
    )(idx4, h.reshape(n, SUBLANES, LANES), gates, u_pack)
    ffn = pl.pallas_call(
        _peer_out_kernel,
        grid=(steps,),
        in_specs=[idx_spec, sel_spec, tab_spec],
        out_specs=tok3_spec,
        out_shape=jax.ShapeDtypeStruct((n, SUBLANES, LANES), F32),
        scratch_shapes=peer_scratch,
        compiler_params=peer_params,
        name="peer_out",
    )(idx4, weights, v_pack)

    out = pl.pallas_call(
        _final_ln_kernel,
        grid=(n // MIX_T,),
        in_specs=[pl.BlockSpec((MIX_T, d), lambda i: (i, 0)), pl.BlockSpec((MIX_T, d), lambda i: (i, 0)),
                  _const_spec((1, d)), _const_spec((1, d))],
        out_specs=pl.BlockSpec((MIX_T, d), lambda i: (i, 0)),
        out_shape=jax.ShapeDtypeStruct((n, d), F32),
        compiler_params=peer_params,
        name="final_ln",
    )(h, ffn.reshape(n, d), row(ln2_w[0]), row(ln2_b[0]))
    return out.reshape(b, s, d)
```

```python
import functools

import jax
import jax.numpy as jnp
from jax import lax
from jax.experimental import pallas as pl
from jax.experimental.pallas import tpu as pltpu

F32 = jnp.float32
BF16 = jnp.bfloat16

LANES = 128
SUBLANES = 8
CHUNK = 128
HEAD_DIM = 128
N_HEADS = 4
GROUP_W = N_HEADS * HEAD_DIM
ROPE_BASE = 10000.0
LN_EPS = 1e-5
GN_EPS = 1e-5
ALPHA = 2.0 ** 0.25

PEER_HEADS = 8
PEER_KEYS = 128
PEER_TOPK = 16
N_SEL = PEER_HEADS * PEER_TOPK
ROWS_PER_EXPERT = 8
PEER_T = 128
PEER_GROUP = 32
MIX_T = 256
ROUTE_T = 512
ROUTE_TILE = 512
VMEM_LIMIT = 56 * 1024 * 1024


def _gelu(x):
    return 0.5 * x * (1.0 + lax.erf(x * (2.0 ** -0.5)))


def _layer_norm(x, w, b, eps):
    mu = jnp.mean(x, axis=-1, keepdims=True)
    xc = x - mu
    var = jnp.mean(xc * xc, axis=-1, keepdims=True)
    return xc * lax.rsqrt(var + eps) * w + b


def _dot(a, b):
    return jnp.dot(a, b, preferred_element_type=F32)


def _dot_nt(a, b):
    return lax.dot_general(a, b, (((1,), (1,)), ((), ())), preferred_element_type=F32)


def _dot_tn(a, b):
    return lax.dot_general(a, b, (((0,), (0,)), ((), ())), preferred_element_type=F32)


def _mix_front_kernel(x_ref, w_in_ref, sln_w_ref, sln_b_ref, ws_ref, bs_ref, cos_ref, sin_ref,
                      kwf_ref, kwb_ref, a_ref, r_ref, sf_ref, sb_ref):
    xb = x_ref[...].astype(BF16)
    chunks = MIX_T // CHUNK

    u = _gelu(_dot(xb, w_in_ref[:, 0:GROUP_W]))
    v = _gelu(_dot(xb, w_in_ref[:, GROUP_W:2 * GROUP_W]))
    vn = _layer_norm(v, sln_w_ref[...], sln_b_ref[...], LN_EPS).astype(BF16)
    for c in range(chunks):
        rows = slice(c * CHUNK, (c + 1) * CHUNK)
        for h in range(N_HEADS):
            cols = slice(h * HEAD_DIM, (h + 1) * HEAD_DIM)
            mixed = _dot(ws_ref[h], vn[rows, cols]) + bs_ref[h]
            a_ref[rows, cols] = (u[rows, cols] * mixed).astype(a_ref.dtype)

    cos = cos_ref[...]
    sin = sin_ref[...]
    q = _dot(xb, w_in_ref[:, 2 * GROUP_W:3 * GROUP_W])
    k = _dot(xb, w_in_ref[:, 3 * GROUP_W:4 * GROUP_W])
    vr = _dot(xb, w_in_ref[:, 4 * GROUP_W:5 * GROUP_W])
    g = _dot(xb, w_in_ref[:, 5 * GROUP_W:6 * GROUP_W])
    r_ref[:, 2 * GROUP_W:3 * GROUP_W] = vr.astype(r_ref.dtype)
    r_ref[:, 3 * GROUP_W:4 * GROUP_W] = (g * jax.nn.sigmoid(g)).astype(r_ref.dtype)
    for h in range(N_HEADS):
        cols = slice(h * HEAD_DIM, (h + 1) * HEAD_DIM)
        qh = q[:, cols]
        kh = k[:, cols]
        qh = qh * cos + pltpu.roll(qh, HEAD_DIM // 2, 1) * sin
        kh = (kh * cos + pltpu.roll(kh, HEAD_DIM // 2, 1) * sin) * (HEAD_DIM ** -0.5)
        r_ref[:, cols] = qh.astype(r_ref.dtype)
        r_ref[:, GROUP_W + h * HEAD_DIM:GROUP_W + (h + 1) * HEAD_DIM] = kh.astype(r_ref.dtype)
        for c in range(chunks):
            rows = slice(c * CHUNK, (c + 1) * CHUNK)
            vc = vr[rows, cols].astype(BF16)
            sf_ref[c, h] = _dot_tn((kh[rows] * kwf_ref[h]).astype(BF16), vc)
            sb_ref[c, h] = _dot_tn((kh[rows] * kwb_ref[h]).astype(BF16), vc)


def _ret_scan_kernel(sf_ref, sb_ref, gf_ref, gb_ref, pf_ref, pb_ref):
    n_chunks = sf_ref.shape[0]
    gf = gf_ref[...]
    gb = gb_ref[...]

    def fwd(c, st):
        pf_ref[c] = st
        return gf * st + sf_ref[c]

    lax.fori_loop(0, n_chunks, fwd, jnp.zeros((HEAD_DIM, HEAD_DIM), F32))

    def bwd(i, st):
        c = n_chunks - 1 - i
        pb_ref[c] = st
        return gb * st + sb_ref[c]

    lax.fori_loop(0, n_chunks, bwd, jnp.zeros((HEAD_DIM, HEAD_DIM), F32))


def _mix_back_kernel(x_ref, a_ref, r_ref, pf_ref, pb_ref, dec_ref, qwf_ref, qwb_ref, gn_ref,
                     w_out_ref, ln_w_ref, ln_b_ref, h_ref, cat_ref):
    chunks = MIX_T // CHUNK
    cat_ref[:, 0:GROUP_W] = a_ref[...]
    for c in range(chunks):
        rows = slice(c * CHUNK, (c + 1) * CHUNK)
        for h in range(N_HEADS):
            cols = slice(h * HEAD_DIM, (h + 1) * HEAD_DIM)
            qh = r_ref[rows, cols]
            kh = r_ref[rows, GROUP_W + h * HEAD_DIM:GROUP_W + (h + 1) * HEAD_DIM]
            vh = r_ref[rows, 2 * GROUP_W + h * HEAD_DIM:2 * GROUP_W + (h + 1) * HEAD_DIM]
            gate = r_ref[rows, 3 * GROUP_W + h * HEAD_DIM:3 * GROUP_W + (h + 1) * HEAD_DIM].astype(F32)
            scores = _dot_nt(qh, kh) * dec_ref[h]
            o = _dot(scores.astype(BF16), vh)
            qf = qh.astype(F32)
            o = o + _dot((qf * qwf_ref[h]).astype(BF16), pf_ref[c, h].astype(BF16))
            o = o + _dot((qf * qwb_ref[h]).astype(BF16), pb_ref[c, h].astype(BF16))
            mu = jnp.mean(o, axis=-1, keepdims=True)
            oc = o - mu
            var = jnp.mean(oc * oc, axis=-1, keepdims=True)
            on = oc * lax.rsqrt(var + GN_EPS) * gn_ref[:, cols]
            cat_ref[rows, GROUP_W + h * HEAD_DIM:GROUP_W + (h + 1) * HEAD_DIM] = (gate * on).astype(cat_ref.dtype)
    mixed = _dot(cat_ref[...], w_out_ref[...])
    h_ref[...] = _layer_norm(ALPHA * x_ref[...] + mixed, ln_w_ref[...], ln_b_ref[...], LN_EPS)


def _top_rows(s, k):
    n_rows = s.shape[0]
    row = lax.broadcasted_iota(jnp.int32, s.shape, 0).astype(F32)
    vals, idxs = [], []
    for _ in range(k):
        m = jnp.max(s, axis=0, keepdims=True)
        i = jnp.min(jnp.where(s == m, row, float(n_rows)), axis=0, keepdims=True)
        vals.append(m)
        idxs.append(i)
        s = jnp.where(row == i, -jnp.inf, s)
    return jnp.concatenate(vals, axis=0), jnp.concatenate(idxs, axis=0).astype(jnp.int32)


def _select_rows(table, pos):
    out = jnp.zeros(pos.shape, table.dtype)
    for a in range(table.shape[0]):
        out = jnp.where(pos == a, table[a:a + 1, :], out)
    return out


def _pair_groups(k):
    groups = []
    a = 0
    while a < k and k // (a + 1) > 1:
        n_b = k // (a + 1)
        for b0 in range(0, n_b, SUBLANES):
            groups.append((a, 0, b0, 1, min(SUBLANES, n_b - b0)))
        a += 1
    for a0 in range(a, k, SUBLANES):
        groups.append((a0, 1, 0, 0, min(SUBLANES, k - a0)))
    return groups


def _route_tile(scores):
    tops, topi = zip(*(_top_rows(sc, PEER_TOPK) for sc in scores))
    sub = lax.broadcasted_iota(jnp.int32, (SUBLANES, scores[0].shape[1]), 0)
    pieces = []
    for a0, a_step, b0, b_step, n_valid in _pair_groups(PEER_TOPK):
        sa = tops[0][a0:a0 + SUBLANES] if a_step else tops[0][a0:a0 + 1]
        sb = tops[1][b0:b0 + SUBLANES] if b_step else tops[1][b0:b0 + 1]
        piece = sa + sb
        pieces.append(piece if n_valid == SUBLANES else jnp.where(sub < n_valid, piece, -jnp.inf))
    best, pos = _top_rows(jnp.concatenate(pieces, axis=0), PEER_TOPK)
    grp = lax.shift_right_logical(pos, 3)
    lane_row = jnp.bitwise_and(pos, SUBLANES - 1)
    pos_a = jnp.zeros_like(pos)
    pos_b = jnp.zeros_like(pos)
    for gi, (a0, a_step, b0, b_step, _) in enumerate(_pair_groups(PEER_TOPK)):
        pos_a = jnp.where(grp == gi, a0 + a_step * lane_row, pos_a)
        pos_b = jnp.where(grp == gi, b0 + b_step * lane_row, pos_b)
    expert = _select_rows(topi[0], pos_a) * PEER_KEYS + _select_rows(topi[1], pos_b)
    e = jnp.exp(best - best[0:1, :])
    return expert, e / jnp.sum(e, axis=0, keepdims=True)


def _peer_route_kernel(h_ref, wq_ref, keys_ref, idx_ref, gate_ref, q_ref):
    head = pl.program_id(1)

    @pl.when(head == 0)
    def _():
        q_ref[...] = _dot(h_ref[...].astype(BF16), wq_ref[...]).astype(q_ref.dtype)

    scores = []
    for p in range(2):
        col = pl.multiple_of((head * 2 + p) * PEER_KEYS, PEER_KEYS)
        scores.append(_dot_nt(keys_ref[p], q_ref[:, pl.ds(col, PEER_KEYS)]))
    for j in range(ROUTE_T // ROUTE_TILE):
        cols = slice(j * ROUTE_TILE, (j + 1) * ROUTE_TILE)
        idx_ref[:, cols], gate_ref[:, cols] = _route_tile([sc[:, cols] for sc in scores])


def _pack_table(tab):
    e, d = tab.shape
    return tab.astype(BF16).reshape(e, ROWS_PER_EXPERT, LANES)


def _slot_mask():
    c = lax.broadcasted_iota(jnp.int32, (SUBLANES, ROWS_PER_EXPERT * N_SEL), 1)
    i = lax.broadcasted_iota(jnp.int32, (SUBLANES, ROWS_PER_EXPERT * N_SEL), 0)
    return i == c % ROWS_PER_EXPERT


def _gather_rows(idx_s, slot, u, tab_ref):
    pieces = []
    for e in range(N_SEL):
        pieces.append(tab_ref[idx_s[slot, u, e]])
    return jnp.concatenate(pieces, axis=0)


def _idx_copy(idx_hbm, idx_s, sem, group, slot):
    rows = idx_hbm.at[pl.ds(group * PEER_GROUP, PEER_GROUP)]
    return pltpu.make_async_copy(rows, idx_s.at[slot], sem.at[slot])


def _group_loop(idx_hbm, idx_s, sem, process):
    step = pl.program_id(0)
    groups = PEER_T // PEER_GROUP
    first = step * groups
    total = pl.num_programs(0) * groups

    @pl.when(step == 0)
    def _():
        _idx_copy(idx_hbm, idx_s, sem, 0, 0).start()

    def pair(jj, carry):
        ga = first + 2 * jj
        _idx_copy(idx_hbm, idx_s, sem, ga + 1, 1).start()
        _idx_copy(idx_hbm, idx_s, sem, ga, 0).wait()
        process(2 * jj * PEER_GROUP, 0)

        @pl.when(ga + 2 < total)
        def _():
            _idx_copy(idx_hbm, idx_s, sem, ga + 2, 0).start()

        _idx_copy(idx_hbm, idx_s, sem, ga + 1, 1).wait()
        process((2 * jj + 1) * PEER_GROUP, 1)
        return carry

    lax.fori_loop(0, groups // 2, pair, 0)


def _peer_act_kernel(idx_hbm, h_ref, gate_ref, tab_ref, w_ref, r_ref, idx_s, sem):
    mask = _slot_mask()

    def process(t0, slot):
        for u in range(PEER_GROUP):
            t = t0 + u
            rows = _gather_rows(idx_s, slot, u, tab_ref)
            r = _dot_nt(h_ref[t].astype(BF16), rows)
            r_ref[pl.ds(t, 1), :] = jnp.sum(jnp.where(mask, r, 0.0), axis=0, keepdims=True)

    _group_loop(idx_hbm, idx_s, sem, process)
    c = lax.broadcasted_iota(jnp.int32, (8 * N_SEL, N_SEL), 0)
    e = lax.broadcasted_iota(jnp.int32, (8 * N_SEL, N_SEL), 1)
    gsel = (c // 8 == e).astype(F32)
    act = jnp.dot(r_ref[...], gsel, precision=lax.Precision.HIGHEST, preferred_element_type=F32)
    w_ref[...] = gate_ref[...] * _gelu(act)


def _peer_out_kernel(idx_hbm, w_ref, tab_ref, o_ref, wx_ref, idx_s, sem):
    mask = _slot_mask()
    e = lax.broadcasted_iota(jnp.int32, (N_SEL, 8 * N_SEL), 0)
    c = lax.broadcasted_iota(jnp.int32, (N_SEL, 8 * N_SEL), 1)
    expand = (c // 8 == e).astype(F32)
    wx_ref[...] = jnp.dot(w_ref[...], expand, precision=lax.Precision.HIGHEST, preferred_element_type=F32)

    def process(t0, slot):
        for u in range(PEER_GROUP):
            t = t0 + u
            rows = _gather_rows(idx_s, slot, u, tab_ref)
            wrow = jnp.broadcast_to(wx_ref[pl.ds(t, 1), :], (SUBLANES, 8 * N_SEL))
            w8 = jnp.where(mask, wrow, 0.0).astype(BF16)
            o_ref[t] = _dot(w8, rows)

    _group_loop(idx_hbm, idx_s, sem, process)


def _final_ln_kernel(h_ref, f_ref, w_ref, b_ref, o_ref):
    o_ref[...] = _layer_norm(ALPHA * h_ref[...] + f_ref[...], w_ref[...], b_ref[...], LN_EPS)


def _const_spec(shape):
    zeros = (0,) * len(shape)
    return pl.BlockSpec(shape, lambda *_: zeros)


def _lane_bcast(v):
    return jnp.broadcast_to(v[:, :, None], v.shape + (LANES,)).astype(F32)


def kernel(x, w_in, sgu_ln_w, sgu_ln_b, sgu_w_s, sgu_b, ret_decay_fwd, ret_decay_bwd, ret_gn_w, w_out, ln1_w, ln1_b, peer_w_q, peer_sub_keys, peer_u, peer_v, ln2_w, ln2_b):
    b, s, d = x.shape
    n = b * s
    n_chunks = s // CHUNK
    assert d == 2 * GROUP_W and s % MIX_T == 0 and n % PEER_T == 0 and n % ROUTE_T == 0
    x2 = x.reshape(n, d)
    row = lambda v: v.reshape(1, -1).astype(F32)

    half = HEAD_DIM // 2
    inv = ROPE_BASE ** (-jnp.arange(half, dtype=F32) / half)
    ang = jnp.arange(s, dtype=F32)[:, None] * inv[None, :]
    cos2 = jnp.concatenate([jnp.cos(ang), jnp.cos(ang)], axis=-1)
    sin2 = jnp.concatenate([-jnp.sin(ang), jnp.sin(ang)], axis=-1)
    lgf = jax.nn.log_sigmoid(ret_decay_fwd[0].astype(F32))
    lgb = jax.nn.log_sigmoid(ret_decay_bwd[0].astype(F32))
    pos = jnp.arange(CHUNK, dtype=F32)
    diff = pos[:, None] - pos[None, :]
    decay = jnp.where(diff[None] >= 0, jnp.exp(jnp.maximum(diff, 0.0)[None] * lgf[:, None, None]),
                      jnp.exp(jnp.maximum(-diff, 0.0)[None] * lgb[:, None, None]))
    kwf = _lane_bcast(jnp.exp((CHUNK - 1 - pos)[None, :] * lgf[:, None]))
    kwb = _lane_bcast(jnp.exp(pos[None, :] * lgb[:, None]))
    qwf = _lane_bcast(jnp.exp((pos + 1.0)[None, :] * lgf[:, None]))
    qwb = _lane_bcast(jnp.exp((CHUNK - pos)[None, :] * lgb[:, None]))
    gcf = jnp.broadcast_to(jnp.exp(CHUNK * lgf)[:, None, None], (N_HEADS, HEAD_DIM, HEAD_DIM)).astype(F32)
    gcb = jnp.broadcast_to(jnp.exp(CHUNK * lgb)[:, None, None], (N_HEADS, HEAD_DIM, HEAD_DIM)).astype(F32)

    mix_grid = (b, s // MIX_T)
    tok_spec = lambda w: pl.BlockSpec((MIX_T, w), lambda i, j: (i * (s // MIX_T) + j, 0))
    pos_spec = pl.BlockSpec((MIX_T, HEAD_DIM), lambda i, j: (j, 0))
    state_spec = pl.BlockSpec((None, MIX_T // CHUNK, N_HEADS, HEAD_DIM, HEAD_DIM), lambda i, j: (i, j, 0, 0, 0))
    head_tab = _const_spec((N_HEADS, CHUNK, HEAD_DIM))
    mix_params = pltpu.CompilerParams(dimension_semantics=("arbitrary", "arbitrary"), vmem_limit_bytes=VMEM_LIMIT)
    state_shape = jax.ShapeDtypeStruct((b, n_chunks, N_HEADS, HEAD_DIM, HEAD_DIM), F32)

    a_out, r_pack, s_f, s_b = pl.pallas_call(
        _mix_front_kernel,
        grid=mix_grid,
        in_specs=[tok_spec(d), _const_spec(w_in.shape[1:]), _const_spec((1, GROUP_W)), _const_spec((1, GROUP_W)),
                  head_tab, head_tab, pos_spec, pos_spec, head_tab, head_tab],
        out_specs=[tok_spec(GROUP_W), tok_spec(4 * GROUP_W), state_spec, state_spec],
        out_shape=[jax.ShapeDtypeStruct((n, GROUP_W), BF16), jax.ShapeDtypeStruct((n, 4 * GROUP_W), BF16),
                   state_shape, state_shape],
        compiler_params=mix_params,
        name="mix_front",
    )(x2, w_in[0].astype(BF16), row(sgu_ln_w[0]), row(sgu_ln_b[0]), sgu_w_s[0].astype(BF16),
      _lane_bcast(sgu_b[0]), cos2, sin2, kwf, kwb)

    scan_state = pl.BlockSpec((None, n_chunks, None, HEAD_DIM, HEAD_DIM), lambda i, j: (i, 0, j, 0, 0))
    scan_tab = pl.BlockSpec((None, HEAD_DIM, HEAD_DIM), lambda i, j: (j, 0, 0))
    p_f, p_b = pl.pallas_call(
        _ret_scan_kernel,
        grid=(b, N_HEADS),
        in_specs=[scan_state, scan_state, scan_tab, scan_tab],
        out_specs=[scan_state, scan_state],
        out_shape=[state_shape, state_shape],
        compiler_params=mix_params,
        name="ret_scan",
    )(s_f, s_b, gcf, gcb)

    h = pl.pallas_call(
        _mix_back_kernel,
        grid=mix_grid,
        in_specs=[tok_spec(d), tok_spec(GROUP_W), tok_spec(4 * GROUP_W), state_spec, state_spec,
                  head_tab, head_tab, head_tab, _const_spec((1, GROUP_W)), _const_spec(w_out.shape[1:]),
                  _const_spec((1, d)), _const_spec((1, d))],
        out_specs=tok_spec(d),
        out_shape=jax.ShapeDtypeStruct((n, d), F32),
        scratch_shapes=[pltpu.VMEM((MIX_T, d), BF16)],
        compiler_params=mix_params,
        name="mix_back",
    )(x2, a_out, r_pack, p_f, p_b, decay.astype(F32), qwf, qwb, row(ret_gn_w[0]), w_out[0].astype(BF16),
      row(ln1_w[0]), row(ln1_b[0]))

    q_width = peer_w_q.shape[-1]
    idx_t, gate_t = pl.pallas_call(
        _peer_route_kernel,
        grid=(n // ROUTE_T, PEER_HEADS),
        in_specs=[pl.BlockSpec((ROUTE_T, d), lambda i, j: (i, 0)),
                  _const_spec((d, q_width)),
                  pl.BlockSpec((None, 2, PEER_KEYS, PEER_KEYS), lambda i, j: (j, 0, 0, 0))],
        out_specs=[pl.BlockSpec((PEER_TOPK, ROUTE_T), lambda i, j: (j, i)),
                   pl.BlockSpec((PEER_TOPK, ROUTE_T), lambda i, j: (j, i))],
        out_shape=[jax.ShapeDtypeStruct((N_SEL, n), jnp.int32), jax.ShapeDtypeStruct((N_SEL, n), F32)],
        scratch_shapes=[pltpu.VMEM((ROUTE_T, q_width), BF16)],
        compiler_params=mix_params,
        name="peer_route",
    )(h, peer_w_q[0].astype(BF16), peer_sub_keys[0].astype(BF16))
    idx4 = idx_t.T
    gates = gate_t.T

    steps = n // PEER_T
    u_pack = _pack_table(peer_u[0])
    v_pack = _pack_table(peer_v[0])
    tab_spec = pl.BlockSpec(u_pack.shape, lambda i: (0, 0, 0), pipeline_mode=pl.Buffered(1))
    idx_spec = pl.BlockSpec(memory_space=pl.ANY)
    sel_spec = pl.BlockSpec((PEER_T, N_SEL), lambda i: (i, 0))
    tok3_spec = pl.BlockSpec((PEER_T, SUBLANES, LANES), lambda i: (i, 0, 0))
    peer_params = pltpu.CompilerParams(dimension_semantics=("arbitrary",), vmem_limit_bytes=VMEM_LIMIT)
    peer_scratch = [pltpu.VMEM((PEER_T, 8 * N_SEL), F32),
                    pltpu.SMEM((2, PEER_GROUP, N_SEL), jnp.int32),
                    pltpu.SemaphoreType.DMA((2,))]
    weights = pl.pallas_call(
        _peer_act_kernel,
        grid=(steps,),
        in_specs=[idx_spec, tok3_spec, sel_spec, tab_spec],
        out_specs=sel_spec,
        out_shape=jax.ShapeDtypeStruct((n, N_SEL), F32),
        scratch_shapes=peer_scratch,
        compiler_params=peer_params,
        name="peer_act",
    )(idx4, h.reshape(n, SUBLANES, LANES), gates, u_pack)
    ffn = pl.pallas_call(
        _peer_out_kernel,
        grid=(steps,),
        in_specs=[idx_spec, sel_spec, tab_spec],
        out_specs=tok3_spec,
        out_shape=jax.ShapeDtypeStruct((n, SUBLANES, LANES), F32),
        scratch_shapes=peer_scratch,
        compiler_params=peer_params,
        name="peer_out",
    )(idx4, weights, v_pack)

    out = pl.pallas_call(
        _final_ln_kernel,
        grid=(n // MIX_T,),
        in_specs=[pl.BlockSpec((MIX_T, d), lambda i: (i, 0)), pl.BlockSpec((MIX_T, d), lambda i: (i, 0)),
                  _const_spec((1, d)), _const_spec((1, d))],
        out_specs=pl.BlockSpec((MIX_T, d), lambda i: (i, 0)),
        out_shape=jax.ShapeDtypeStruct((n, d), F32),
        compiler_params=peer_params,
        name="final_ln",
    )(h, ffn.reshape(n, d), row(ln2_w[0]), row(ln2_b[0]))
    return out.reshape(b, s, d)
```

```python
import functools

import jax
import jax.numpy as jnp
from jax import lax
from jax.experimental import pallas as pl
from jax.experimental.pallas import tpu as pltpu

F32 = jnp.float32
BF16 = jnp.bfloat16

LANES = 128
SUBLANES = 8
CHUNK = 128
HEAD_DIM = 128
N_HEADS = 4
GROUP_W = N_HEADS * HEAD_DIM
ROPE_BASE = 10000.0
LN_EPS = 1e-5
GN_EPS = 1e-5
ALPHA = 2.0 ** 0.25

PEER_HEADS = 8
PEER_KEYS = 128
PEER_TOPK = 16
N_SEL = PEER_HEADS * PEER_TOPK
ROWS_PER_EXPERT = 8
PEER_T = 128
PEER_GROUP = 64
MIX_T = 256
ROUTE_T = 512
ROUTE_TILE = 512
VMEM_LIMIT = 56 * 1024 * 1024


def _gelu(x):
    return 0.5 * x * (1.0 + lax.erf(x * (2.0 ** -0.5)))


def _layer_norm(x, w, b, eps):
    mu = jnp.mean(x, axis=-1, keepdims=True)
    xc = x - mu
    var = jnp.mean(xc * xc, axis=-1, keepdims=True)
    return xc * lax.rsqrt(var + eps) * w + b


def _dot(a, b):
    return jnp.dot(a, b, preferred_element_type=F32)


def _dot_nt(a, b):
    return lax.dot_general(a, b, (((1,), (1,)), ((), ())), preferred_element_type=F32)


def _dot_tn(a, b):
    return lax.dot_general(a, b, (((0,), (0,)), ((), ())), preferred_element_type=F32)


def _mix_front_kernel(x_ref, w_in_ref, sln_w_ref, sln_b_ref, ws_ref, bs_ref, cos_ref, sin_ref,
                      kwf_ref, kwb_ref, a_ref, r_ref, sf_ref, sb_ref):
    xb = x_ref[...].astype(BF16)
    chunks = MIX_T // CHUNK

    u = _gelu(_dot(xb, w_in_ref[:, 0:GROUP_W]))
    v = _gelu(_dot(xb, w_in_ref[:, GROUP_W:2 * GROUP_W]))
    vn = _layer_norm(v, sln_w_ref[...], sln_b_ref[...], LN_EPS).astype(BF16)
    for c in range(chunks):
        rows = slice(c * CHUNK, (c + 1) * CHUNK)
        for h in range(N_HEADS):
            cols = slice(h * HEAD_DIM, (h + 1) * HEAD_DIM)
            mixed = _dot(ws_ref[h], vn[rows, cols]) + bs_ref[h]
            a_ref[rows, cols] = (u[rows, cols] * mixed).astype(a_ref.dtype)

    cos = cos_ref[...]
    sin = sin_ref[...]
    q = _dot(xb, w_in_ref[:, 2 * GROUP_W:3 * GROUP_W])
    k = _dot(xb, w_in_ref[:, 3 * GROUP_W:4 * GROUP_W])
    vr = _dot(xb, w_in_ref[:, 4 * GROUP_W:5 * GROUP_W])
    g = _dot(xb, w_in_ref[:, 5 * GROUP_W:6 * GROUP_W])
    r_ref[:, 2 * GROUP_W:3 * GROUP_W] = vr.astype(r_ref.dtype)
    r_ref[:, 3 * GROUP_W:4 * GROUP_W] = (g * jax.nn.sigmoid(g)).astype(r_ref.dtype)
    for h in range(N_HEADS):
        cols = slice(h * HEAD_DIM, (h + 1) * HEAD_DIM)
        qh = q[:, cols]
        kh = k[:, cols]
        qh = qh * cos + pltpu.roll(qh, HEAD_DIM // 2, 1) * sin
        kh = (kh * cos + pltpu.roll(kh, HEAD_DIM // 2, 1) * sin) * (HEAD_DIM ** -0.5)
        r_ref[:, cols] = qh.astype(r_ref.dtype)
        r_ref[:, GROUP_W + h * HEAD_DIM:GROUP_W + (h + 1) * HEAD_DIM] = kh.astype(r_ref.dtype)
        for c in range(chunks):
            rows = slice(c * CHUNK, (c + 1) * CHUNK)
            vc = vr[rows, cols].astype(BF16)
            sf_ref[c, h] = _dot_tn((kh[rows] * kwf_ref[h]).astype(BF16), vc)
            sb_ref[c, h] = _dot_tn((kh[rows] * kwb_ref[h]).astype(BF16), vc)


def _ret_scan_kernel(sf_ref, sb_ref, gf_ref, gb_ref, pf_ref, pb_ref):
    n_chunks = sf_ref.shape[0]
    gf = gf_ref[...]
    gb = gb_ref[...]

    def fwd(c, st):
        pf_ref[c] = st
        return gf * st + sf_ref[c]

    lax.fori_loop(0, n_chunks, fwd, jnp.zeros((HEAD_DIM, HEAD_DIM), F32))

    def bwd(i, st):
        c = n_chunks - 1 - i
        pb_ref[c] = st
        return gb * st + sb_ref[c]

    lax.fori_loop(0, n_chunks, bwd, jnp.zeros((HEAD_DIM, HEAD_DIM), F32))


def _mix_back_kernel(x_ref, a_ref, r_ref, pf_ref, pb_ref, dec_ref, qwf_ref, qwb_ref, gn_ref,
                     w_out_ref, ln_w_ref, ln_b_ref, h_ref, cat_ref):
    chunks = MIX_T // CHUNK
    cat_ref[:, 0:GROUP_W] = a_ref[...]
    for c in range(chunks):
        rows = slice(c * CHUNK, (c + 1) * CHUNK)
        for h in range(N_HEADS):
            cols = slice(h * HEAD_DIM, (h + 1) * HEAD_DIM)
            qh = r_ref[rows, cols]
            kh = r_ref[rows, GROUP_W + h * HEAD_DIM:GROUP_W + (h + 1) * HEAD_DIM]
            vh = r_ref[rows, 2 * GROUP_W + h * HEAD_DIM:2 * GROUP_W + (h + 1) * HEAD_DIM]
            gate = r_ref[rows, 3 * GROUP_W + h * HEAD_DIM:3 * GROUP_W + (h + 1) * HEAD_DIM].astype(F32)
            scores = _dot_nt(qh, kh) * dec_ref[h]
            o = _dot(scores.astype(BF16), vh)
            qf = qh.astype(F32)
            o = o + _dot((qf * qwf_ref[h]).astype(BF16), pf_ref[c, h].astype(BF16))
            o = o + _dot((qf * qwb_ref[h]).astype(BF16), pb_ref[c, h].astype(BF16))
            mu = jnp.mean(o, axis=-1, keepdims=True)
            oc = o - mu
            var = jnp.mean(oc * oc, axis=-1, keepdims=True)
            on = oc * lax.rsqrt(var + GN_EPS) * gn_ref[:, cols]
            cat_ref[rows, GROUP_W + h * HEAD_DIM:GROUP_W + (h + 1) * HEAD_DIM] = (gate * on).astype(cat_ref.dtype)
    mixed = _dot(cat_ref[...], w_out_ref[...])
    h_ref[...] = _layer_norm(ALPHA * x_ref[...] + mixed, ln_w_ref[...], ln_b_ref[...], LN_EPS)


def _top_rows(s, k):
    n_rows = s.shape[0]
    row = lax.broadcasted_iota(jnp.int32, s.shape, 0).astype(F32)
    vals, idxs = [], []
    for _ in range(k):
        m = jnp.max(s, axis=0, keepdims=True)
        i = jnp.min(jnp.where(s == m, row, float(n_rows)), axis=0, keepdims=True)
        vals.append(m)
        idxs.append(i)
        s = jnp.where(row == i, -jnp.inf, s)
    return jnp.concatenate(vals, axis=0), jnp.concatenate(idxs, axis=0).astype(jnp.int32)


def _select_rows(table, pos):
    out = jnp.zeros(pos.shape, table.dtype)
    for a in range(table.shape[0]):
        out = jnp.where(pos == a, table[a:a + 1, :], out)
    return out


def _pair_groups(k):
    groups = []
    a = 0
    while a < k and k // (a + 1) > 1:
        n_b = k // (a + 1)
        for b0 in range(0, n_b, SUBLANES):
            groups.append((a, 0, b0, 1, min(SUBLANES, n_b - b0)))
        a += 1
    for a0 in range(a, k, SUBLANES):
        groups.append((a0, 1, 0, 0, min(SUBLANES, k - a0)))
    return groups


def _route_tile(scores):
    tops, topi = zip(*(_top_rows(sc, PEER_TOPK) for sc in scores))
    sub = lax.broadcasted_iota(jnp.int32, (SUBLANES, scores[0].shape[1]), 0)
    pieces = []
    for a0, a_step, b0, b_step, n_valid in _pair_groups(PEER_TOPK):
        sa = tops[0][a0:a0 + SUBLANES] if a_step else tops[0][a0:a0 + 1]
        sb = tops[1][b0:b0 + SUBLANES] if b_step else tops[1][b0:b0 + 1]
        piece = sa + sb
        pieces.append(piece if n_valid == SUBLANES else jnp.where(sub < n_valid, piece, -jnp.inf))
    best, pos = _top_rows(jnp.concatenate(pieces, axis=0), PEER_TOPK)
    grp = lax.shift_right_logical(pos, 3)
    lane_row = jnp.bitwise_and(pos, SUBLANES - 1)
    pos_a = jnp.zeros_like(pos)
    pos_b = jnp.zeros_like(pos)
    for gi, (a0, a_step, b0, b_step, _) in enumerate(_pair_groups(PEER_TOPK)):
        pos_a = jnp.where(grp == gi, a0 + a_step * lane_row, pos_a)
        pos_b = jnp.where(grp == gi, b0 + b_step * lane_row, pos_b)
    expert = _select_rows(topi[0], pos_a) * PEER_KEYS + _select_rows(topi[1], pos_b)
    e = jnp.exp(best - best[0:1, :])
    return expert, e / jnp.sum(e, axis=0, keepdims=True)


def _peer_route_kernel(h_ref, wq_ref, keys_ref, idx_ref, gate_ref, q_ref):
    head = pl.program_id(1)

    @pl.when(head == 0)
    def _():
        q_ref[...] = _dot(h_ref[...].astype(BF16), wq_ref[...]).astype(q_ref.dtype)

    scores = []
    for p in range(2):
        col = pl.multiple_of((head * 2 + p) * PEER_KEYS, PEER_KEYS)
        scores.append(_dot_nt(keys_ref[p], q_ref[:, pl.ds(col, PEER_KEYS)]))
    for j in range(ROUTE_T // ROUTE_TILE):
        cols = slice(j * ROUTE_TILE, (j + 1) * ROUTE_TILE)
        idx_ref[:, cols], gate_ref[:, cols] = _route_tile([sc[:, cols] for sc in scores])


def _pack_table(tab):
    e, d = tab.shape
    return tab.astype(BF16).reshape(e, ROWS_PER_EXPERT, LANES)


def _slot_mask():
    c = lax.broadcasted_iota(jnp.int32, (SUBLANES, ROWS_PER_EXPERT * N_SEL), 1)
    i = lax.broadcasted_iota(jnp.int32, (SUBLANES, ROWS_PER_EXPERT * N_SEL), 0)
    return i == c % ROWS_PER_EXPERT


def _gather_rows(idx_s, slot, u, tab_ref):
    pieces = []
    for e in range(N_SEL):
        pieces.append(tab_ref[idx_s[slot, u, e]])
    return jnp.concatenate(pieces, axis=0)


def _idx_copy(idx_hbm, idx_s, sem, group, slot):
    rows = idx_hbm.at[pl.ds(group * PEER_GROUP, PEER_GROUP)]
    return pltpu.make_async_copy(rows, idx_s.at[slot], sem.at[slot])


def _group_loop(idx_hbm, idx_s, sem, process):
    step = pl.program_id(0)
    groups = PEER_T // PEER_GROUP
    first = step * groups
    total = pl.num_programs(0) * groups

    @pl.when(step == 0)
    def _():
        _idx_copy(idx_hbm, idx_s, sem, 0, 0).start()

    def pair(jj, carry):
        ga = first + 2 * jj
        _idx_copy(idx_hbm, idx_s, sem, ga + 1, 1).start()
        _idx_copy(idx_hbm, idx_s, sem, ga, 0).wait()
        process(2 * jj * PEER_GROUP, 0)

        @pl.when(ga + 2 < total)
        def _():
            _idx_copy(idx_hbm, idx_s, sem, ga + 2, 0).start()

        _idx_copy(idx_hbm, idx_s, sem, ga + 1, 1).wait()
        process((2 * jj + 1) * PEER_GROUP, 1)
        return carry

    lax.fori_loop(0, groups // 2, pair, 0)


def _peer_act_kernel(idx_hbm, h_ref, gate_ref, tab_ref, w_ref, r_ref, idx_s, sem):
    mask = _slot_mask()

    def process(t0, slot):
        for u in range(PEER_GROUP):
            t = t0 + u
            rows = _gather_rows(idx_s, slot, u, tab_ref)
            r = _dot_nt(h_ref[t].astype(BF16), rows)
            r_ref[pl.ds(t, 1), :] = jnp.sum(jnp.where(mask, r, 0.0), axis=0, keepdims=True)

    _group_loop(idx_hbm, idx_s, sem, process)
    c = lax.broadcasted_iota(jnp.int32, (8 * N_SEL, N_SEL), 0)
    e = lax.broadcasted_iota(jnp.int32, (8 * N_SEL, N_SEL), 1)
    gsel = (c // 8 == e).astype(F32)
    act = jnp.dot(r_ref[...], gsel, precision=lax.Precision.HIGHEST, preferred_element_type=F32)
    w_ref[...] = gate_ref[...] * _gelu(act)


def _peer_out_kernel(idx_hbm, w_ref, tab_ref, o_ref, wx_ref, idx_s, sem):
    mask = _slot_mask()
    e = lax.broadcasted_iota(jnp.int32, (N_SEL, 8 * N_SEL), 0)
    c = lax.broadcasted_iota(jnp.int32, (N_SEL, 8 * N_SEL), 1)
    expand = (c // 8 == e).astype(F32)
    wx_ref[...] = jnp.dot(w_ref[...], expand, precision=lax.Precision.HIGHEST, preferred_element_type=F32)

    def process(t0, slot):
        for u in range(PEER_GROUP):
            t = t0 + u
            rows = _gather_rows(idx_s, slot, u, tab_ref)
            wrow = jnp.broadcast_to(wx_ref[pl.ds(t, 1), :], (SUBLANES, 8 * N_SEL))
            w8 = jnp.where(mask, wrow, 0.0).astype(BF16)
            o_ref[t] = _dot(w8, rows)

    _group_loop(idx_hbm, idx_s, sem, process)


def _final_ln_kernel(h_ref, f_ref, w_ref, b_ref, o_ref):
    o_ref[...] = _layer_norm(ALPHA * h_ref[...] + f_ref[...], w_ref[...], b_ref[...], LN_EPS)


def _const_spec(shape):
    zeros = (0,) * len(shape)
    return pl.BlockSpec(shape, lambda *_: zeros)


def _lane_bcast(v):
    return jnp.broadcast_to(v[:, :, None], v.shape + (LANES,)).astype(F32)


def kernel(x, w_in, sgu_ln_w, sgu_ln_b, sgu_w_s, sgu_b, ret_decay_fwd, ret_decay_bwd, ret_gn_w, w_out, ln1_w, ln1_b, peer_w_q, peer_sub_keys, peer_u, peer_v, ln2_w, ln2_b):
    b, s, d = x.shape
    n = b * s
    n_chunks = s // CHUNK
    assert d == 2 * GROUP_W and s % MIX_T == 0 and n % PEER_T == 0 and n % ROUTE_T == 0
    x2 = x.reshape(n, d)
    row = lambda v: v.reshape(1, -1).astype(F32)

    half = HEAD_DIM // 2
    inv = ROPE_BASE ** (-jnp.arange(half, dtype=F32) / half)
    ang = jnp.arange(s, dtype=F32)[:, None] * inv[None, :]
    cos2 = jnp.concatenate([jnp.cos(ang), jnp.cos(ang)], axis=-1)
    sin2 = jnp.concatenate([-jnp.sin(ang), jnp.sin(ang)], axis=-1)
    lgf = jax.nn.log_sigmoid(ret_decay_fwd[0].astype(F32))
    lgb = jax.nn.log_sigmoid(ret_decay_bwd[0].astype(F32))
    pos = jnp.arange(CHUNK, dtype=F32)
    diff = pos[:, None] - pos[None, :]
    decay = jnp.where(diff[None] >= 0, jnp.exp(jnp.maximum(diff, 0.0)[None] * lgf[:, None, None]),
                      jnp.exp(jnp.maximum(-diff, 0.0)[None] * lgb[:, None, None]))
    kwf = _lane_bcast(jnp.exp((CHUNK - 1 - pos)[None, :] * lgf[:, None]))
    kwb = _lane_bcast(jnp.exp(pos[None, :] * lgb[:, None]))
    qwf = _lane_bcast(jnp.exp((pos + 1.0)[None, :] * lgf[:, None]))
    qwb = _lane_bcast(jnp.exp((CHUNK - pos)[None, :] * lgb[:, None]))
    gcf = jnp.broadcast_to(jnp.exp(CHUNK * lgf)[:, None, None], (N_HEADS, HEAD_DIM, HEAD_DIM)).astype(F32)
    gcb = jnp.broadcast_to(jnp.exp(CHUNK * lgb)[:, None, None], (N_HEADS, HEAD_DIM, HEAD_DIM)).astype(F32)

    mix_grid = (b, s // MIX_T)
    tok_spec = lambda w: pl.BlockSpec((MIX_T, w), lambda i, j: (i * (s // MIX_T) + j, 0))
    pos_spec = pl.BlockSpec((MIX_T, HEAD_DIM), lambda i, j: (j, 0))
    state_spec = pl.BlockSpec((None, MIX_T // CHUNK, N_HEADS, HEAD_DIM, HEAD_DIM), lambda i, j: (i, j, 0, 0, 0))
    head_tab = _const_spec((N_HEADS, CHUNK, HEAD_DIM))
    mix_params = pltpu.CompilerParams(dimension_semantics=("arbitrary", "arbitrary"), vmem_limit_bytes=VMEM_LIMIT)
    state_shape = jax.ShapeDtypeStruct((b, n_chunks, N_HEADS, HEAD_DIM, HEAD_DIM), F32)

    a_out, r_pack, s_f, s_b = pl.pallas_call(
        _mix_front_kernel,
        grid=mix_grid,
        in_specs=[tok_spec(d), _const_spec(w_in.shape[1:]), _const_spec((1, GROUP_W)), _const_spec((1, GROUP_W)),
                  head_tab, head_tab, pos_spec, pos_spec, head_tab, head_tab],
        out_specs=[tok_spec(GROUP_W), tok_spec(4 * GROUP_W), state_spec, state_spec],
        out_shape=[jax.ShapeDtypeStruct((n, GROUP_W), BF16), jax.ShapeDtypeStruct((n, 4 * GROUP_W), BF16),
                   state_shape, state_shape],
        compiler_params=mix_params,
        name="mix_front",
    )(x2, w_in[0].astype(BF16), row(sgu_ln_w[0]), row(sgu_ln_b[0]), sgu_w_s[0].astype(BF16),
      _lane_bcast(sgu_b[0]), cos2, sin2, kwf, kwb)

    scan_state = pl.BlockSpec((None, n_chunks, None, HEAD_DIM, HEAD_DIM), lambda i, j: (i, 0, j, 0, 0))
    scan_tab = pl.BlockSpec((None, HEAD_DIM, HEAD_DIM), lambda i, j: (j, 0, 0))
    p_f, p_b = pl.pallas_call(
        _ret_scan_kernel,
        grid=(b, N_HEADS),
        in_specs=[scan_state, scan_state, scan_tab, scan_tab],
        out_specs=[scan_state, scan_state],
        out_shape=[state_shape, state_shape],
        compiler_params=mix_params,
        name="ret_scan",
    )(s_f, s_b, gcf, gcb)

    h = pl.pallas_call(
        _mix_back_kernel,
        grid=mix_grid,
        in_specs=[tok_spec(d), tok_spec(GROUP_W), tok_spec(4 * GROUP_W), state_spec, state_spec,
                  head_tab, head_tab, head_tab, _const_spec((1, GROUP_W)), _const_spec(w_out.shape[1:]),
                  _const_spec((1, d)), _const_spec((1, d))],
        out_specs=tok_spec(d),
        out_shape=jax.ShapeDtypeStruct((n, d), F32),
        scratch_shapes=[pltpu.VMEM((MIX_T, d), BF16)],
        compiler_params=mix_params,
        name="mix_back",
    )(x2, a_out, r_pack, p_f, p_b, decay.astype(F32), qwf, qwb, row(ret_gn_w[0]), w_out[0].astype(BF16),
      row(ln1_w[0]), row(ln1_b[0]))

    q_width = peer_w_q.shape[-1]
    idx_t, gate_t = pl.pallas_call(
        _peer_route_kernel,
        grid=(n // ROUTE_T, PEER_HEADS),
        in_specs=[pl.BlockSpec((ROUTE_T, d), lambda i, j: (i, 0)),
                  _const_spec((d, q_width)),
                  pl.BlockSpec((None, 2, PEER_KEYS, PEER_KEYS), lambda i, j: (j, 0, 0, 0))],
        out_specs=[pl.BlockSpec((PEER_TOPK, ROUTE_T), lambda i, j: (j, i)),
                   pl.BlockSpec((PEER_TOPK, ROUTE_T), lambda i, j: (j, i))],
        out_shape=[jax.ShapeDtypeStruct((N_SEL, n), jnp.int32), jax.ShapeDtypeStruct((N_SEL, n), F32)],
        scratch_shapes=[pltpu.VMEM((ROUTE_T, q_width), BF16)],
        compiler_params=mix_params,
        name="peer_route",
    )(h, peer_w_q[0].astype(BF16), peer_sub_keys[0].astype(BF16))
    idx4 = idx_t.T
    gates = gate_t.T

    steps = n // PEER_T
    u_pack = _pack_table(peer_u[0])
    v_pack = _pack_table(peer_v[0])
    tab_spec = pl.BlockSpec(u_pack.shape, lambda i: (0, 0, 0), pipeline_mode=pl.Buffered(1))
    idx_spec = pl.BlockSpec(memory_space=pl.ANY)
    sel_spec = pl.BlockSpec((PEER_T, N_SEL), lambda i: (i, 0))
    tok3_spec = pl.BlockSpec((PEER_T, SUBLANES, LANES), lambda i: (i, 0, 0))
    peer_params = pltpu.CompilerParams(dimension_semantics=("arbitrary",), vmem_limit_bytes=VMEM_LIMIT)
    peer_scratch = [pltpu.VMEM((PEER_T, 8 * N_SEL), F32),
                    pltpu.SMEM((2, PEER_GROUP, N_SEL), jnp.int32),
                    pltpu.SemaphoreType.DMA((2,))]
    weights = pl.pallas_call(
        _peer_act_kernel,
        grid=(steps,),
        in_specs=[idx_spec, tok3_spec, sel_spec, tab_spec],
        out_specs=sel_spec,
        out_shape=jax.ShapeDtypeStruct((n, N_SEL), F32),
        scratch_shapes=peer_scratch,
        compiler_params=peer_params,
        name="peer_act",
    )(idx4, h.reshape(n, SUBLANES, LANES), gates, u_pack)
    ffn = pl.pallas_call(
        _peer_out_kernel,
        grid=(steps,),
        in_specs=[idx_spec, sel_spec, tab_spec],
        out_specs=tok3_spec,
        out_shape=jax.ShapeDtypeStruct((n, SUBLANES, LANES), F32),
        scratch_shapes=peer_scratch,
        compiler_params=peer_params,
        name="peer_out",
    )(idx4, weights, v_pack)

    out = pl.pallas_call(
        _final_ln_kernel,
        grid=(n // MIX_T,),
        in_specs=[pl.BlockSpec((MIX_T, d), lambda i: (i, 0)), pl.BlockSpec((MIX_T, d), lambda i: (i, 0)),
                  _const_spec((1, d)), _const_spec((1, d))],
        out_specs=pl.BlockSpec((MIX_T, d), lambda i: (i, 0)),
        out_shape=jax.ShapeDtypeStruct((n, d), F32),
        compiler_params=peer_params,
        name="final_ln",
    )(h, ffn.reshape(n, d), row(ln2_w[0]), row(ln2_b[0]))
    return out.reshape(b, s, d)
```

```python
import functools

import jax
import jax.numpy as jnp
from jax import lax
from jax.experimental import pallas as pl
from jax.experimental.pallas import tpu as pltpu

F32 = jnp.float32
BF16 = jnp.bfloat16

LANES = 128
SUBLANES = 8
CHUNK = 128
HEAD_DIM = 128
N_HEADS = 4
GROUP_W = N_HEADS * HEAD_DIM
ROPE_BASE = 10000.0
LN_EPS = 1e-5
GN_EPS = 1e-5
ALPHA = 2.0 ** 0.25

PEER_HEADS = 8
PEER_KEYS = 128
PEER_TOPK = 16
N_SEL = PEER_HEADS * PEER_TOPK
ROWS_PER_EXPERT = 8
PEER_T = 128
PEER_GROUP = 64
MIX_T = 256
ROUTE_T = 1024
ROUTE_TILE = 1024
VMEM_LIMIT = 56 * 1024 * 1024


def _gelu(x):
    return 0.5 * x * (1.0 + lax.erf(x * (2.0 ** -0.5)))


def _layer_norm(x, w, b, eps):
    mu = jnp.mean(x, axis=-1, keepdims=True)
    xc = x - mu
    var = jnp.mean(xc * xc, axis=-1, keepdims=True)
    return xc * lax.rsqrt(var + eps) * w + b


def _dot(a, b):
    return jnp.dot(a, b, preferred_element_type=F32)


def _dot_nt(a, b):
    return lax.dot_general(a, b, (((1,), (1,)), ((), ())), preferred_element_type=F32)


def _dot_tn(a, b):
    return lax.dot_general(a, b, (((0,), (0,)), ((), ())), preferred_element_type=F32)


def _mix_front_kernel(x_ref, w_in_ref, sln_w_ref, sln_b_ref, ws_ref, bs_ref, cos_ref, sin_ref,
                      kwf_ref, kwb_ref, a_ref, r_ref, sf_ref, sb_ref):
    xb = x_ref[...].astype(BF16)
    chunks = MIX_T // CHUNK

    u = _gelu(_dot(xb, w_in_ref[:, 0:GROUP_W]))
    v = _gelu(_dot(xb, w_in_ref[:, GROUP_W:2 * GROUP_W]))
    vn = _layer_norm(v, sln_w_ref[...], sln_b_ref[...], LN_EPS).astype(BF16)
    for c in range(chunks):
        rows = slice(c * CHUNK, (c + 1) * CHUNK)
        for h in range(N_HEADS):
            cols = slice(h * HEAD_DIM, (h + 1) * HEAD_DIM)
            mixed = _dot(ws_ref[h], vn[rows, cols]) + bs_ref[h]
            a_ref[rows, cols] = (u[rows, cols] * mixed).astype(a_ref.dtype)

    cos = cos_ref[...]
    sin = sin_ref[...]
    q = _dot(xb, w_in_ref[:, 2 * GROUP_W:3 * GROUP_W])
    k = _dot(xb, w_in_ref[:, 3 * GROUP_W:4 * GROUP_W])
    vr = _dot(xb, w_in_ref[:, 4 * GROUP_W:5 * GROUP_W])
    g = _dot(xb, w_in_ref[:, 5 * GROUP_W:6 * GROUP_W])
    r_ref[:, 2 * GROUP_W:3 * GROUP_W] = vr.astype(r_ref.dtype)
    r_ref[:, 3 * GROUP_W:4 * GROUP_W] = (g * jax.nn.sigmoid(g)).astype(r_ref.dtype)
    for h in range(N_HEADS):
        cols = slice(h * HEAD_DIM, (h + 1) * HEAD_DIM)
        qh = q[:, cols]
        kh = k[:, cols]
        qh = qh * cos + pltpu.roll(qh, HEAD_DIM // 2, 1) * sin
        kh = (kh * cos + pltpu.roll(kh, HEAD_DIM // 2, 1) * sin) * (HEAD_DIM ** -0.5)
        r_ref[:, cols] = qh.astype(r_ref.dtype)
        r_ref[:, GROUP_W + h * HEAD_DIM:GROUP_W + (h + 1) * HEAD_DIM] = kh.astype(r_ref.dtype)
        for c in range(chunks):
            rows = slice(c * CHUNK, (c + 1) * CHUNK)
            vc = vr[rows, cols].astype(BF16)
            sf_ref[c, h] = _dot_tn((kh[rows] * kwf_ref[h]).astype(BF16), vc)
            sb_ref[c, h] = _dot_tn((kh[rows] * kwb_ref[h]).astype(BF16), vc)


def _ret_scan_kernel(sf_ref, sb_ref, gf_ref, gb_ref, pf_ref, pb_ref):
    n_chunks = sf_ref.shape[0]
    gf = gf_ref[...]
    gb = gb_ref[...]

    def fwd(c, st):
        pf_ref[c] = st
        return gf * st + sf_ref[c]

    lax.fori_loop(0, n_chunks, fwd, jnp.zeros((HEAD_DIM, HEAD_DIM), F32))

    def bwd(i, st):
        c = n_chunks - 1 - i
        pb_ref[c] = st
        return gb * st + sb_ref[c]

    lax.fori_loop(0, n_chunks, bwd, jnp.zeros((HEAD_DIM, HEAD_DIM), F32))


def _mix_back_kernel(x_ref, a_ref, r_ref, pf_ref, pb_ref, dec_ref, qwf_ref, qwb_ref, gn_ref,
                     w_out_ref, ln_w_ref, ln_b_ref, h_ref, cat_ref):
    chunks = MIX_T // CHUNK
    cat_ref[:, 0:GROUP_W] = a_ref[...]
    for c in range(chunks):
        rows = slice(c * CHUNK, (c + 1) * CHUNK)
        for h in range(N_HEADS):
            cols = slice(h * HEAD_DIM, (h + 1) * HEAD_DIM)
            qh = r_ref[rows, cols]
            kh = r_ref[rows, GROUP_W + h * HEAD_DIM:GROUP_W + (h + 1) * HEAD_DIM]
            vh = r_ref[rows, 2 * GROUP_W + h * HEAD_DIM:2 * GROUP_W + (h + 1) * HEAD_DIM]
            gate = r_ref[rows, 3 * GROUP_W + h * HEAD_DIM:3 * GROUP_W + (h + 1) * HEAD_DIM].astype(F32)
            scores = _dot_nt(qh, kh) * dec_ref[h]
            o = _dot(scores.astype(BF16), vh)
            qf = qh.astype(F32)
            o = o + _dot((qf * qwf_ref[h]).astype(BF16), pf_ref[c, h].astype(BF16))
            o = o + _dot((qf * qwb_ref[h]).astype(BF16), pb_ref[c, h].astype(BF16))
            mu = jnp.mean(o, axis=-1, keepdims=True)
            oc = o - mu
            var = jnp.mean(oc * oc, axis=-1, keepdims=True)
            on = oc * lax.rsqrt(var + GN_EPS) * gn_ref[:, cols]
            cat_ref[rows, GROUP_W + h * HEAD_DIM:GROUP_W + (h + 1) * HEAD_DIM] = (gate * on).astype(cat_ref.dtype)
    mixed = _dot(cat_ref[...], w_out_ref[...])
    h_ref[...] = _layer_norm(ALPHA * x_ref[...] + mixed, ln_w_ref[...], ln_b_ref[...], LN_EPS)


def _top_rows(s, k):
    n_rows = s.shape[0]
    row = lax.broadcasted_iota(jnp.int32, s.shape, 0).astype(F32)
    vals, idxs = [], []
    for _ in range(k):
        m = jnp.max(s, axis=0, keepdims=True)
        i = jnp.min(jnp.where(s == m, row, float(n_rows)), axis=0, keepdims=True)
        vals.append(m)
        idxs.append(i)
        s = jnp.where(row == i, -jnp.inf, s)
    return jnp.concatenate(vals, axis=0), jnp.concatenate(idxs, axis=0).astype(jnp.int32)


def _select_rows(table, pos):
    out = jnp.zeros(pos.shape, table.dtype)
    for a in range(table.shape[0]):
        out = jnp.where(pos == a, table[a:a + 1, :], out)
    return out


def _pair_groups(k):
    groups = []
    a = 0
    while a < k and k // (a + 1) > 1:
        n_b = k // (a + 1)
        for b0 in range(0, n_b, SUBLANES):
            groups.append((a, 0, b0, 1, min(SUBLANES, n_b - b0)))
        a += 1
    for a0 in range(a, k, SUBLANES):
        groups.append((a0, 1, 0, 0, min(SUBLANES, k - a0)))
    return groups


def _route_tile(scores):
    tops, topi = zip(*(_top_rows(sc, PEER_TOPK) for sc in scores))
    sub = lax.broadcasted_iota(jnp.int32, (SUBLANES, scores[0].shape[1]), 0)
    pieces = []
    for a0, a_step, b0, b_step, n_valid in _pair_groups(PEER_TOPK):
        sa = tops[0][a0:a0 + SUBLANES] if a_step else tops[0][a0:a0 + 1]
        sb = tops[1][b0:b0 + SUBLANES] if b_step else tops[1][b0:b0 + 1]
        piece = sa + sb
        pieces.append(piece if n_valid == SUBLANES else jnp.where(sub < n_valid, piece, -jnp.inf))
    best, pos = _top_rows(jnp.concatenate(pieces, axis=0), PEER_TOPK)
    grp = lax.shift_right_logical(pos, 3)
    lane_row = jnp.bitwise_and(pos, SUBLANES - 1)
    pos_a = jnp.zeros_like(pos)
    pos_b = jnp.zeros_like(pos)
    for gi, (a0, a_step, b0, b_step, _) in enumerate(_pair_groups(PEER_TOPK)):
        pos_a = jnp.where(grp == gi, a0 + a_step * lane_row, pos_a)
        pos_b = jnp.where(grp == gi, b0 + b_step * lane_row, pos_b)
    expert = _select_rows(topi[0], pos_a) * PEER_KEYS + _select_rows(topi[1], pos_b)
    e = jnp.exp(best - best[0:1, :])
    return expert, e / jnp.sum(e, axis=0, keepdims=True)


def _peer_route_kernel(h_ref, wq_ref, keys_ref, idx_ref, gate_ref, q_ref):
    head = pl.program_id(1)

    @pl.when(head == 0)
    def _():
        q_ref[...] = _dot(h_ref[...].astype(BF16), wq_ref[...]).astype(q_ref.dtype)

    scores = []
    for p in range(2):
        col = pl.multiple_of((head * 2 + p) * PEER_KEYS, PEER_KEYS)
        scores.append(_dot_nt(keys_ref[p], q_ref[:, pl.ds(col, PEER_KEYS)]))
    for j in range(ROUTE_T // ROUTE_TILE):
        cols = slice(j * ROUTE_TILE, (j + 1) * ROUTE_TILE)
        idx_ref[:, cols], gate_ref[:, cols] = _route_tile([sc[:, cols] for sc in scores])


def _table_tiles(tab):
    e, d = tab.shape
    return tab.astype(BF16).reshape(e, ROWS_PER_EXPERT, LANES)


def _slot_mask():
    c = lax.broadcasted_iota(jnp.int32, (SUBLANES, ROWS_PER_EXPERT * N_SEL), 1)
    i = lax.broadcasted_iota(jnp.int32, (SUBLANES, ROWS_PER_EXPERT * N_SEL), 0)
    return i == c % ROWS_PER_EXPERT


def _gather_rows(idx_s, slot, u, tab_ref):
    pieces = []
    for e in range(N_SEL):
        pieces.append(tab_ref[idx_s[slot, u, e]])
    return jnp.concatenate(pieces, axis=0)


def _idx_copy(idx_hbm, idx_s, sem, group, slot):
    rows = idx_hbm.at[pl.ds(group * PEER_GROUP, PEER_GROUP)]
    return pltpu.make_async_copy(rows, idx_s.at[slot], sem.at[slot])


def _group_loop(idx_hbm, idx_s, sem, process):
    step = pl.program_id(0)
    groups = PEER_T // PEER_GROUP
    first = step * groups
    total = pl.num_programs(0) * groups

    @pl.when(step == 0)
    def _():
        _idx_copy(idx_hbm, idx_s, sem, 0, 0).start()

    def pair(jj, carry):
        ga = first + 2 * jj
        _idx_copy(idx_hbm, idx_s, sem, ga + 1, 1).start()
        _idx_copy(idx_hbm, idx_s, sem, ga, 0).wait()
        process(2 * jj * PEER_GROUP, 0)

        @pl.when(ga + 2 < total)
        def _():
            _idx_copy(idx_hbm, idx_s, sem, ga + 2, 0).start()

        _idx_copy(idx_hbm, idx_s, sem, ga + 1, 1).wait()
        process((2 * jj + 1) * PEER_GROUP, 1)
        return carry

    lax.fori_loop(0, groups // 2, pair, 0)


def _peer_act_kernel(idx_hbm, h_ref, gate_ref, tab_ref, w_ref, r_ref, idx_s, sem):
    mask = _slot_mask()

    def process(t0, slot):
        for u in range(PEER_GROUP):
            t = t0 + u
            rows = _gather_rows(idx_s, slot, u, tab_ref)
            r = _dot_nt(h_ref[t].astype(BF16), rows)
            r_ref[pl.ds(t, 1), :] = jnp.sum(jnp.where(mask, r, 0.0), axis=0, keepdims=True)

    _group_loop(idx_hbm, idx_s, sem, process)
    c = lax.broadcasted_iota(jnp.int32, (8 * N_SEL, N_SEL), 0)
    e = lax.broadcasted_iota(jnp.int32, (8 * N_SEL, N_SEL), 1)
    gsel = (c // 8 == e).astype(F32)
    act = jnp.dot(r_ref[...], gsel, precision=lax.Precision.HIGHEST, preferred_element_type=F32)
    w_ref[...] = gate_ref[...] * _gelu(act)


def _peer_out_kernel(idx_hbm, w_ref, tab_ref, o_ref, wx_ref, idx_s, sem):
    mask = _slot_mask()
    e = lax.broadcasted_iota(jnp.int32, (N_SEL, 8 * N_SEL), 0)
    c = lax.broadcasted_iota(jnp.int32, (N_SEL, 8 * N_SEL), 1)
    expand = (c // 8 == e).astype(F32)
    wx_ref[...] = jnp.dot(w_ref[...], expand, precision=lax.Precision.HIGHEST, preferred_element_type=F32)

    def process(t0, slot):
        for u in range(PEER_GROUP):
            t = t0 + u
            rows = _gather_rows(idx_s, slot, u, tab_ref)
            wrow = jnp.broadcast_to(wx_ref[pl.ds(t, 1), :], (SUBLANES, 8 * N_SEL))
            w8 = jnp.where(mask, wrow, 0.0).astype(BF16)
            o_ref[t] = _dot(w8, rows)

    _group_loop(idx_hbm, idx_s, sem, process)


def _final_ln_kernel(h_ref, f_ref, w_ref, b_ref, o_ref):
    o_ref[...] = _layer_norm(ALPHA * h_ref[...] + f_ref[...], w_ref[...], b_ref[...], LN_EPS)


def _const_spec(shape):
    zeros = (0,) * len(shape)
    return pl.BlockSpec(shape, lambda *_: zeros)


def _lane_bcast(v):
    return jnp.broadcast_to(v[:, :, None], v.shape + (LANES,)).astype(F32)


def kernel(x, w_in, sgu_ln_w, sgu_ln_b, sgu_w_s, sgu_b, ret_decay_fwd, ret_decay_bwd, ret_gn_w, w_out, ln1_w, ln1_b, peer_w_q, peer_sub_keys, peer_u, peer_v, ln2_w, ln2_b):
    b, s, d = x.shape
    n = b * s
    n_chunks = s // CHUNK
    assert d == 2 * GROUP_W and s % MIX_T == 0 and n % PEER_T == 0 and n % ROUTE_T == 0
    assert PEER_T % (2 * PEER_GROUP) == 0 and ROUTE_T % ROUTE_TILE == 0
    x2 = x.reshape(n, d)
    row = lambda v: v.reshape(1, -1).astype(F32)

    half = HEAD_DIM // 2
    inv = ROPE_BASE ** (-jnp.arange(half, dtype=F32) / half)
    ang = jnp.arange(s, dtype=F32)[:, None] * inv[None, :]
    cos2 = jnp.concatenate([jnp.cos(ang), jnp.cos(ang)], axis=-1)
    sin2 = jnp.concatenate([-jnp.sin(ang), jnp.sin(ang)], axis=-1)
    lgf = jax.nn.log_sigmoid(ret_decay_fwd[0].astype(F32))
    lgb = jax.nn.log_sigmoid(ret_decay_bwd[0].astype(F32))
    pos = jnp.arange(CHUNK, dtype=F32)
    diff = pos[:, None] - pos[None, :]
    decay = jnp.where(diff[None] >= 0, jnp.exp(jnp.maximum(diff, 0.0)[None] * lgf[:, None, None]),
                      jnp.exp(jnp.maximum(-diff, 0.0)[None] * lgb[:, None, None]))
    kwf = _lane_bcast(jnp.exp((CHUNK - 1 - pos)[None, :] * lgf[:, None]))
    kwb = _lane_bcast(jnp.exp(pos[None, :] * lgb[:, None]))
    qwf = _lane_bcast(jnp.exp((pos + 1.0)[None, :] * lgf[:, None]))
    qwb = _lane_bcast(jnp.exp((CHUNK - pos)[None, :] * lgb[:, None]))
    gcf = jnp.broadcast_to(jnp.exp(CHUNK * lgf)[:, None, None], (N_HEADS, HEAD_DIM, HEAD_DIM)).astype(F32)
    gcb = jnp.broadcast_to(jnp.exp(CHUNK * lgb)[:, None, None], (N_HEADS, HEAD_DIM, HEAD_DIM)).astype(F32)

    mix_grid = (b, s // MIX_T)
    tok_spec = lambda w: pl.BlockSpec((MIX_T, w), lambda i, j: (i * (s // MIX_T) + j, 0))
    pos_spec = pl.BlockSpec((MIX_T, HEAD_DIM), lambda i, j: (j, 0))
    state_spec = pl.BlockSpec((None, MIX_T // CHUNK, N_HEADS, HEAD_DIM, HEAD_DIM), lambda i, j: (i, j, 0, 0, 0))
    head_tab = _const_spec((N_HEADS, CHUNK, HEAD_DIM))
    mix_params = pltpu.CompilerParams(dimension_semantics=("arbitrary", "arbitrary"), vmem_limit_bytes=VMEM_LIMIT)
    state_shape = jax.ShapeDtypeStruct((b, n_chunks, N_HEADS, HEAD_DIM, HEAD_DIM), F32)

    a_out, r_pack, s_f, s_b = pl.pallas_call(
        _mix_front_kernel,
        grid=mix_grid,
        in_specs=[tok_spec(d), _const_spec(w_in.shape[1:]), _const_spec((1, GROUP_W)), _const_spec((1, GROUP_W)),
                  head_tab, head_tab, pos_spec, pos_spec, head_tab, head_tab],
        out_specs=[tok_spec(GROUP_W), tok_spec(4 * GROUP_W), state_spec, state_spec],
        out_shape=[jax.ShapeDtypeStruct((n, GROUP_W), BF16), jax.ShapeDtypeStruct((n, 4 * GROUP_W), BF16),
                   state_shape, state_shape],
        compiler_params=mix_params,
        name="mix_front",
    )(x2, w_in[0].astype(BF16), row(sgu_ln_w[0]), row(sgu_ln_b[0]), sgu_w_s[0].astype(BF16),
      _lane_bcast(sgu_b[0]), cos2, sin2, kwf, kwb)

    scan_state = pl.BlockSpec((None, n_chunks, None, HEAD_DIM, HEAD_DIM), lambda i, j: (i, 0, j, 0, 0))
    scan_tab = pl.BlockSpec((None, HEAD_DIM, HEAD_DIM), lambda i, j: (j, 0, 0))
    p_f, p_b = pl.pallas_call(
        _ret_scan_kernel,
        grid=(b, N_HEADS),
        in_specs=[scan_state, scan_state, scan_tab, scan_tab],
        out_specs=[scan_state, scan_state],
        out_shape=[state_shape, state_shape],
        compiler_params=mix_params,
        name="ret_scan",
    )(s_f, s_b, gcf, gcb)

    h = pl.pallas_call(
        _mix_back_kernel,
        grid=mix_grid,
        in_specs=[tok_spec(d), tok_spec(GROUP_W), tok_spec(4 * GROUP_W), state_spec, state_spec,
                  head_tab, head_tab, head_tab, _const_spec((1, GROUP_W)), _const_spec(w_out.shape[1:]),
                  _const_spec((1, d)), _const_spec((1, d))],
        out_specs=tok_spec(d),
        out_shape=jax.ShapeDtypeStruct((n, d), F32),
        scratch_shapes=[pltpu.VMEM((MIX_T, d), BF16)],
        compiler_params=mix_params,
        name="mix_back",
    )(x2, a_out, r_pack, p_f, p_b, decay.astype(F32), qwf, qwb, row(ret_gn_w[0]), w_out[0].astype(BF16),
      row(ln1_w[0]), row(ln1_b[0]))

    q_width = peer_w_q.shape[-1]
    idx_t, gate_t = pl.pallas_call(
        _peer_route_kernel,
        grid=(n // ROUTE_T, PEER_HEADS),
        in_specs=[pl.BlockSpec((ROUTE_T, d), lambda i, j: (i, 0)),
                  _const_spec((d, q_width)),
                  pl.BlockSpec((None, 2, PEER_KEYS, PEER_KEYS), lambda i, j: (j, 0, 0, 0))],
        out_specs=[pl.BlockSpec((PEER_TOPK, ROUTE_T), lambda i, j: (j, i)),
                   pl.BlockSpec((PEER_TOPK, ROUTE_T), lambda i, j: (j, i))],
        out_shape=[jax.ShapeDtypeStruct((N_SEL, n), jnp.int32), jax.ShapeDtypeStruct((N_SEL, n), F32)],
        scratch_shapes=[pltpu.VMEM((ROUTE_T, q_width), BF16)],
        compiler_params=mix_params,
        name="peer_route",
    )(h, peer_w_q[0].astype(BF16), peer_sub_keys[0].astype(BF16))
    expert_ids = idx_t.T
    gates = gate_t.T

    steps = n // PEER_T
    u_tiles = _table_tiles(peer_u[0])
    v_tiles = _table_tiles(peer_v[0])
    tab_spec = pl.BlockSpec(u_tiles.shape, lambda i: (0, 0, 0), pipeline_mode=pl.Buffered(1))
    idx_spec = pl.BlockSpec(memory_space=pl.ANY)
    sel_spec = pl.BlockSpec((PEER_T, N_SEL), lambda i: (i, 0))
    tok3_spec = pl.BlockSpec((PEER_T, SUBLANES, LANES), lambda i: (i, 0, 0))
    peer_params = pltpu.CompilerParams(dimension_semantics=("arbitrary",), vmem_limit_bytes=VMEM_LIMIT)
    peer_scratch = [pltpu.VMEM((PEER_T, 8 * N_SEL), F32),
                    pltpu.SMEM((2, PEER_GROUP, N_SEL), jnp.int32),
                    pltpu.SemaphoreType.DMA((2,))]
    weights = pl.pallas_call(
        _peer_act_kernel,
        grid=(steps,),
        in_specs=[idx_spec, tok3_spec, sel_spec, tab_spec],
        out_specs=sel_spec,
        out_shape=jax.ShapeDtypeStruct((n, N_SEL), F32),
        scratch_shapes=peer_scratch,
        compiler_params=peer_params,
        name="peer_act",
    )(expert_ids, h.reshape(n, SUBLANES, LANES), gates, u_tiles)
    ffn = pl.pallas_call(
        _peer_out_kernel,
        grid=(steps,),
        in_specs=[idx_spec, sel_spec, tab_spec],
        out_specs=tok3_spec,
        out_shape=jax.ShapeDtypeStruct((n, SUBLANES, LANES), F32),
        scratch_shapes=peer_scratch,
        compiler_params=peer_params,
        name="peer_out",
    )(expert_ids, weights, v_tiles)

    out = pl.pallas_call(
        _final_ln_kernel,
        grid=(n // MIX_T,),
        in_specs=[pl.BlockSpec((MIX_T, d), lambda i: (i, 0)), pl.BlockSpec((MIX_T, d), lambda i: (i, 0)),
                  _const_spec((1, d)), _const_spec((1, d))],
        out_specs=pl.BlockSpec((MIX_T, d), lambda i: (i, 0)),
        out_shape=jax.ShapeDtypeStruct((n, d), F32),
        compiler_params=peer_params,
        name="final_ln",
    )(h, ffn.reshape(n, d), row(ln2_w[0]), row(ln2_b[0]))
    return out.reshape(b, s, d)
```
